```python
import math
import jax, jax.numpy as jnp
from jax import lax
import numpy as np

D_MODEL = 1024
BATCH = 8
SEQ = 4096
DEPTH = 1
DEC_BATCH = 8
DEC_SEQ = 2048
PAST_LEN = 128

N_META = 16
HEAD_DIM = 64
N_HEADS = 8
N_KV = 2
GROUP = N_HEADS // N_KV
ATTN_WIDTH = N_HEADS * HEAD_DIM
KV_WIDTH = N_KV * HEAD_DIM
WINDOW = 128
BLOCK = 128
ROPE_THETA = 10000.0
HY_WIDTH = D_MODEL // 2
HY_ORDER = 2
HY_PROJ = (HY_ORDER + 1) * HY_WIDTH
SHORT_CONV = 3
FILT_EMB = 33
FILT_BANDS = (FILT_EMB - 1) // 2
FILT_HIDDEN = 64
FILT_OUT = 2 * HY_ORDER * HY_WIDTH
DECAY_TARGET = 1e-2
FAST_DECAY_PCT = 0.3
SLOW_DECAY_PCT = 1.5
N_EXPERTS = 16
CAPACITY_FACTOR = 2
D_EXPERT = 2048
EPS = 1e-6
IN_COLS = ATTN_WIDTH + 2 * KV_WIDTH + HY_PROJ + 2 * D_MODEL
SPLITS = [ATTN_WIDTH, ATTN_WIDTH + KV_WIDTH, ATTN_WIDTH + 2 * KV_WIDTH,
          ATTN_WIDTH + 2 * KV_WIDTH + HY_PROJ, ATTN_WIDTH + 2 * KV_WIDTH + HY_PROJ + D_MODEL]

kernel_name = 'hybrid_swa_hyena_ec_moe_encoder'

F32 = jnp.float32


def _rmsnorm(x, g):
    x32 = x.astype(F32)
    y = x32 * lax.rsqrt(jnp.mean(x32 * x32, axis=-1, keepdims=True) + EPS) * g.astype(F32)
    return y.astype(x.dtype)


def _rope(x):
    L = x.shape[1]
    half = HEAD_DIM // 2
    inv = 1.0 / (ROPE_THETA ** (jnp.arange(half, dtype=F32) / half))
    ang = jnp.arange(L, dtype=F32)[:, None] * inv[None]
    cos = jnp.cos(ang)[None, :, None, :]
    sin = jnp.sin(ang)[None, :, None, :]
    x1 = x[..., :half].astype(F32)
    x2 = x[..., half:].astype(F32)
    return jnp.concatenate([x1 * cos - x2 * sin, x2 * cos + x1 * sin], -1).astype(x.dtype)


def _window_attention(q, k, v, sink):
    B, L = q.shape[:2]
    Lp = -(-L // BLOCK) * BLOCK
    nb = Lp // BLOCK
    q = jnp.pad(q, ((0, 0), (0, Lp - L), (0, 0), (0, 0)))
    kvpad = ((0, 0), (BLOCK, Lp - L + BLOCK), (0, 0), (0, 0))
    kp = jnp.pad(k, kvpad).reshape(B, nb + 2, BLOCK, N_KV, HEAD_DIM)
    vp = jnp.pad(v, kvpad).reshape(B, nb + 2, BLOCK, N_KV, HEAD_DIM)
    kb = jnp.concatenate([kp[:, :-2], kp[:, 1:-1], kp[:, 2:]], axis=2)
    vb = jnp.concatenate([vp[:, :-2], vp[:, 1:-1], vp[:, 2:]], axis=2)
    qb = q.reshape(B, nb, BLOCK, N_KV, GROUP, HEAD_DIM)
    s = jnp.einsum('bnqkgd,bnckd->bnkgqc', qb, kb, preferred_element_type=F32) * (HEAD_DIM ** -0.5)
    qpos = jnp.arange(nb)[:, None] * BLOCK + jnp.arange(BLOCK)[None]
    kpos = (jnp.arange(nb)[:, None] - 1) * BLOCK + jnp.arange(3 * BLOCK)[None]
    rel = kpos[:, None, :] - qpos[:, :, None]
    mask = (jnp.abs(rel) <= WINDOW) & (kpos[:, None, :] >= 0) & (kpos[:, None, :] < L)
    s = jnp.where(mask[None, :, None, None], s, -jnp.inf)
    sk = sink.astype(F32).reshape(N_KV, GROUP)[None, None, :, :, None, None]
    m = jnp.maximum(jnp.max(s, axis=-1, keepdims=True), sk)
    p = jnp.exp(s - m)
    denom = jnp.sum(p, axis=-1, keepdims=True) + jnp.exp(sk - m)
    o = jnp.einsum('bnkgqc,bnckd->bnqkgd', (p / denom).astype(v.dtype), vb)
    return o.reshape(B, Lp, ATTN_WIDTH)[:, :L]


def _hyena_filter_spectra(L, w1, b1, w2, b2, w3, b3, freq, wout):
    n = jnp.arange(L, dtype=F32)
    t = n / max(L - 1, 1)
    f = jnp.linspace(1e-4, FILT_BANDS - 1, FILT_BANDS, dtype=F32)
    ang = (2.0 * math.pi / L) * n[:, None] * f[None]
    z = jnp.concatenate([t[:, None], jnp.cos(ang), -jnp.sin(ang)], axis=-1)
    fr = freq.astype(F32)
    a = jnp.sin(fr * (z @ w1.astype(F32) + b1.astype(F32)))
    a = jnp.sin(fr * (a @ w2.astype(F32) + b2.astype(F32)))
    a = jnp.sin(fr * (a @ w3.astype(F32) + b3.astype(F32)))
    h = (a @ wout.astype(F32)).reshape(L, 2, HY_ORDER, HY_WIDTH)
    max_decay = math.log(DECAY_TARGET) / FAST_DECAY_PCT
    min_decay = math.log(DECAY_TARGET) / SLOW_DECAY_PCT
    deltas = jnp.abs(jnp.linspace(min_decay, max_decay, HY_WIDTH, dtype=F32))
    h = h * jnp.exp(-t[:, None] * deltas[None])[:, None, None, :]
    hf, hb = h[:, 0], h[:, 1]
    circ = jnp.concatenate([hf[:1] + hb[:1], hf[1:], jnp.zeros_like(hf[:1]), hb[:0:-1]], axis=0)
    return jnp.fft.rfft(circ, axis=0)


def _long_conv(z, kf):
    L = z.shape[1]
    zf = jnp.fft.rfft(z, n=2 * L, axis=1)
    return jnp.fft.irfft(zf * kf[None], n=2 * L, axis=1)[:, :L]


def _hyena(u, conv_w, conv_b, kf, skip):
    up = jnp.pad(u, ((0, 0), (1, 1), (0, 0)))
    uc = conv_w[0] * up[:, :-2] + conv_w[1] * up[:, 1:-1] + conv_w[2] * up[:, 2:] + conv_b
    v, x1, x2 = jnp.split(uc, HY_ORDER + 1, axis=-1)
    z = v.astype(F32)
    sk = skip.astype(F32)
    for o, gate in enumerate((x1, x2)):
        z = gate.astype(F32) * (_long_conv(z, kf[:, o]) + sk[o] * z)
    return z.astype(u.dtype)


def _expert_choice(hn, w_router, w_gate, w_up, w_down):
    B, L, D = hn.shape
    t = hn.reshape(B * L, D)
    n = B * L
    cap = max(1, (CAPACITY_FACTOR * n) // N_EXPERTS)
    aff = jax.nn.softmax(jnp.matmul(t, w_router, preferred_element_type=F32), axis=-1)
    g, idx = lax.top_k(aff.T, cap)
    xe = t[idx]
    hid = jax.nn.silu(jnp.einsum('ecd,edf->ecf', xe, w_gate)) * jnp.einsum('ecd,edf->ecf', xe, w_up)
    ye = jnp.einsum('ecf,efd->ecd', hid, w_down) * g[..., None].astype(t.dtype)
    out = jnp.zeros_like(t).at[idx.reshape(-1)].add(ye.reshape(-1, D))
    return out.reshape(B, L, D)


def _encode(x, meta_tokens, norm_mix_g, w_in, attn_sink, hy_conv_w, hy_conv_b,
            hy_filt_w1, hy_filt_b1, hy_filt_w2, hy_filt_b2, hy_filt_w3, hy_filt_b3,
            hy_filt_freq, hy_filt_wout, hy_skip, w_attn_up, w_hyena_up, w_out,
            norm_ffn_g, w_router, w_exp_gate, w_exp_up, w_exp_down, norm_final_g):
    B = x.shape[0]
    meta = jnp.broadcast_to(meta_tokens.astype(x.dtype)[None], (B, N_META, D_MODEL))
    h = jnp.concatenate([meta, x], axis=1)
    L = h.shape[1]
    for i in range(DEPTH):
        n = _rmsnorm(h, norm_mix_g[i])
        proj = n @ w_in[i]
        q, k, v, hy, g_a, g_h = jnp.split(proj, SPLITS, axis=-1)
        q = _rope(q.reshape(B, L, N_HEADS, HEAD_DIM))
        k = _rope(k.reshape(B, L, N_KV, HEAD_DIM))
        v = v.reshape(B, L, N_KV, HEAD_DIM)
        a_out = _window_attention(q, k, v, attn_sink[i])
        kf = _hyena_filter_spectra(L, hy_filt_w1[i], hy_filt_b1[i], hy_filt_w2[i], hy_filt_b2[i],
                                   hy_filt_w3[i], hy_filt_b3[i], hy_filt_freq[i], hy_filt_wout[i])
        h_out = _hyena(hy, hy_conv_w[i], hy_conv_b[i], kf, hy_skip[i])
        merged = jax.nn.sigmoid(g_a) * (a_out @ w_attn_up[i]) + jax.nn.sigmoid(g_h) * (h_out @ w_hyena_up[i])
        h = h + merged @ w_out[i]
        h = h + _expert_choice(_rmsnorm(h, norm_ffn_g[i]), w_router[i], w_exp_gate[i], w_exp_up[i], w_exp_down[i])
    return _rmsnorm(h, norm_final_g)[:, N_META:]


def setup_inputs(seed: int = 0) -> dict:
    key = jax.random.key(seed)
    ks = jax.random.split(key, 32)
    nrm = lambda k, shape, scale: jax.random.normal(k, shape, F32) * scale
    return {
        'x_prompt': nrm(ks[0], (BATCH, SEQ, D_MODEL), 1.0),
        'x_sample': nrm(ks[1], (DEC_BATCH, DEC_SEQ, D_MODEL), 1.0),
        'meta_tokens': nrm(ks[2], (N_META, D_MODEL), 1.0),
        'norm_mix_g': 1.0 + nrm(ks[3], (DEPTH, D_MODEL), 0.01),
        'w_in': nrm(ks[4], (DEPTH, D_MODEL, IN_COLS), D_MODEL ** -0.5),
        'attn_sink': nrm(ks[5], (DEPTH, N_HEADS), 0.5),
        'hy_conv_w': nrm(ks[6], (DEPTH, SHORT_CONV, HY_PROJ), SHORT_CONV ** -0.5),
        'hy_conv_b': nrm(ks[7], (DEPTH, HY_PROJ), 0.01),
        'hy_filt_w1': nrm(ks[8], (DEPTH, FILT_EMB, FILT_HIDDEN), FILT_EMB ** -0.5),
        'hy_filt_b1': nrm(ks[9], (DEPTH, FILT_HIDDEN), 0.1),
        'hy_filt_w2': nrm(ks[10], (DEPTH, FILT_HIDDEN, FILT_HIDDEN), FILT_HIDDEN ** -0.5),
        'hy_filt_b2': nrm(ks[11], (DEPTH, FILT_HIDDEN), 0.1),
        'hy_filt_w3': nrm(ks[12], (DEPTH, FILT_HIDDEN, FILT_HIDDEN), FILT_HIDDEN ** -0.5),
        'hy_filt_b3': nrm(ks[13], (DEPTH, FILT_HIDDEN), 0.1),
        'hy_filt_freq': 1.0 + nrm(ks[14], (DEPTH, FILT_HIDDEN), 0.01),
        'hy_filt_wout': nrm(ks[15], (DEPTH, FILT_HIDDEN, FILT_OUT), 0.05 * FILT_HIDDEN ** -0.5),
        'hy_skip': nrm(ks[16], (DEPTH, HY_ORDER, HY_WIDTH), 0.5),
        'w_attn_up': nrm(ks[17], (DEPTH, ATTN_WIDTH, D_MODEL), ATTN_WIDTH ** -0.5),
        'w_hyena_up': nrm(ks[18], (DEPTH, HY_WIDTH, D_MODEL), HY_WIDTH ** -0.5),
        'w_out': nrm(ks[19], (DEPTH, D_MODEL, D_MODEL), D_MODEL ** -0.5),
        'norm_ffn_g': 1.0 + nrm(ks[20], (DEPTH, D_MODEL), 0.01),
        'w_router': nrm(ks[21], (DEPTH, D_MODEL, N_EXPERTS), D_MODEL ** -0.5),
        'w_exp_gate': nrm(ks[22], (DEPTH, N_EXPERTS, D_MODEL, D_EXPERT), D_MODEL ** -0.5),
        'w_exp_up': nrm(ks[23], (DEPTH, N_EXPERTS, D_MODEL, D_EXPERT), D_MODEL ** -0.5),
        'w_exp_down': nrm(ks[24], (DEPTH, N_EXPERTS, D_EXPERT, D_MODEL), D_EXPERT ** -0.5),
        'norm_final_g': 1.0 + nrm(ks[25], (D_MODEL,), 0.01),
    }


def reference(x_prompt, x_sample, meta_tokens, norm_mix_g, w_in, attn_sink, hy_conv_w, hy_conv_b,
              hy_filt_w1, hy_filt_b1, hy_filt_w2, hy_filt_b2, hy_filt_w3, hy_filt_b3,
              hy_filt_freq, hy_filt_wout, hy_skip, w_attn_up, w_hyena_up, w_out,
              norm_ffn_g, w_router, w_exp_gate, w_exp_up, w_exp_down, norm_final_g):
    y_prompt = _encode(x_prompt, meta_tokens, norm_mix_g, w_in, attn_sink, hy_conv_w, hy_conv_b,
                       hy_filt_w1, hy_filt_b1, hy_filt_w2, hy_filt_b2, hy_filt_w3, hy_filt_b3,
                       hy_filt_freq, hy_filt_wout, hy_skip, w_attn_up, w_hyena_up, w_out,
                       norm_ffn_g, w_router, w_exp_gate, w_exp_up, w_exp_down, norm_final_g)
    y_sample = _encode(x_sample, meta_tokens, norm_mix_g, w_in, attn_sink, hy_conv_w, hy_conv_b,
                       hy_filt_w1, hy_filt_b1, hy_filt_w2, hy_filt_b2, hy_filt_w3, hy_filt_b3,
                       hy_filt_freq, hy_filt_wout, hy_skip, w_attn_up, w_hyena_up, w_out,
                       norm_ffn_g, w_router, w_exp_gate, w_exp_up, w_exp_down, norm_final_g)
    return (y_prompt, y_sample)
```

```python
import functools
import math

import jax
import jax.numpy as jnp
from jax import lax
from jax.experimental import pallas as pl
from jax.experimental.pallas import tpu as pltpu

F32 = jnp.float32
BF16 = jnp.bfloat16

D_MODEL = 1024
N_META = 16
HEAD_DIM = 64
N_HEADS = 8
N_KV = 2
GROUP = N_HEADS // N_KV
ATTN_WIDTH = N_HEADS * HEAD_DIM
KV_WIDTH = N_KV * HEAD_DIM
WINDOW = 128
BLOCK = 128
ROPE_THETA = 10000.0
HY_WIDTH = D_MODEL // 2
HY_ORDER = 2
HY_PROJ = (HY_ORDER + 1) * HY_WIDTH
FILT_EMB = 33
FILT_BANDS = (FILT_EMB - 1) // 2
FILT_HIDDEN = 64
DECAY_TARGET = 1e-2
FAST_DECAY_PCT = 0.3
SLOW_DECAY_PCT = 1.5
N_EXPERTS = 16
CAPACITY_FACTOR = 2
D_EXPERT = 2048
EPS = 1e-6
Q0, K0, V0, HY0, GA0, GH0 = 0, ATTN_WIDTH, ATTN_WIDTH + KV_WIDTH, ATTN_WIDTH + 2 * KV_WIDTH, \
    ATTN_WIDTH + 2 * KV_WIDTH + HY_PROJ, ATTN_WIDTH + 2 * KV_WIDTH + HY_PROJ + D_MODEL
IN_COLS = GH0 + D_MODEL

LANES = 128
ROUTER_PAD = LANES
VMEM_LIMIT = 56 * 1024 * 1024
NEG_BIG = -1e30


def _cparams(sem):
    return pltpu.CompilerParams(dimension_semantics=sem, vmem_limit_bytes=VMEM_LIMIT)


def _pick_tile(n, target, mult):
    best = None
    for t in range(mult, min(n, target) + 1, mult):
        if n % t == 0:
            best = t
    assert best is not None, (n, target, mult)
    return best


def _rms(x, g):
    return x * lax.rsqrt(jnp.mean(x * x, axis=-1, keepdims=True) + EPS) * g


def _inproj_kernel(x_ref, g_ref, w_ref, q_ref, kv_ref, hy_ref, gate_ref):
    xn = _rms(x_ref[...], g_ref[...]).astype(BF16)
    q_ref[...] = jnp.dot(xn, w_ref[:, Q0:K0], preferred_element_type=F32)
    kv_ref[...] = jnp.dot(xn, w_ref[:, K0:HY0], preferred_element_type=F32)
    hy_ref[...] = jnp.dot(xn, w_ref[:, HY0:GA0], preferred_element_type=F32)
    gate_ref[...] = jnp.dot(xn, w_ref[:, GA0:IN_COLS], preferred_element_type=F32)


def _inproj(h2, g, w):
    rows = h2.shape[0]
    tm = _pick_tile(rows, 256, 128)
    row = lambda w_: pl.BlockSpec((tm, w_), lambda i: (i, 0))
    return pl.pallas_call(
        _inproj_kernel,
        grid=(rows // tm,),
        in_specs=[row(D_MODEL), pl.BlockSpec((1, D_MODEL), lambda i: (0, 0)),
                  pl.BlockSpec((D_MODEL, IN_COLS), lambda i: (0, 0))],
        out_specs=[row(ATTN_WIDTH), row(2 * KV_WIDTH), row(HY_PROJ), row(2 * D_MODEL)],
        out_shape=[jax.ShapeDtypeStruct((rows, ATTN_WIDTH), F32),
                   jax.ShapeDtypeStruct((rows, 2 * KV_WIDTH), F32),
                   jax.ShapeDtypeStruct((rows, HY_PROJ), F32),
                   jax.ShapeDtypeStruct((rows, 2 * D_MODEL), F32)],
        compiler_params=_cparams(("parallel",)),
        name="inproj",
    )(h2, g, w)


def _rope(x, cos, sin_signed):
    width = x.shape[-1]
    reps = width // LANES
    if reps > 1:
        cos = jnp.concatenate([cos] * reps, axis=-1)
        sin_signed = jnp.concatenate([sin_signed] * reps, axis=-1)
    lane = lax.broadcasted_iota(jnp.int32, x.shape, 1)
    first_half = (lane % HEAD_DIM) < (HEAD_DIM // 2)
    rot = jnp.where(first_half, pltpu.roll(x, width - HEAD_DIM // 2, 1), pltpu.roll(x, HEAD_DIM // 2, 1))
    return x * cos + rot * sin_signed


def _attn_kernel(sink_ref, q_ref, kvp_ref, kvc_ref, kvn_ref, cq_ref, sq_ref, cp_ref, sp_ref, cn_ref, sn_ref,
                 o_ref, *, seq_len, padded_len):
    i = pl.program_id(1)
    nb = pl.num_programs(1)
    q = (_rope(q_ref[0], cq_ref[...], sq_ref[...]) * (HEAD_DIM ** -0.5)).astype(BF16)
    k = jnp.concatenate([
        _rope(kvp_ref[0, :, :KV_WIDTH], cp_ref[...], sp_ref[...]),
        _rope(kvc_ref[0, :, :KV_WIDTH], cq_ref[...], sq_ref[...]),
        _rope(kvn_ref[0, :, :KV_WIDTH], cn_ref[...], sn_ref[...])], axis=0).astype(BF16)
    v = jnp.concatenate([kvp_ref[0, :, KV_WIDTH:], kvc_ref[0, :, KV_WIDTH:], kvn_ref[0, :, KV_WIDTH:]],
                        axis=0).astype(BF16)
    rq = i * BLOCK + lax.broadcasted_iota(jnp.int32, (BLOCK, 1), 0)
    pos_q = (rq + N_META) % padded_len
    c = lax.broadcasted_iota(jnp.int32, (1, 3 * BLOCK), 1)
    blk = (i + nb - 1 + c // BLOCK) % nb
    pos_k = (blk * BLOCK + c % BLOCK + N_META) % padded_len
    mask = (jnp.abs(pos_k - pos_q) <= WINDOW) & (pos_k < seq_len)
    for h in range(N_HEADS):
        kh = h // GROUP
        qh = q[:, h * HEAD_DIM:(h + 1) * HEAD_DIM]
        kk = k[:, kh * HEAD_DIM:(kh + 1) * HEAD_DIM]
        vv = v[:, kh * HEAD_DIM:(kh + 1) * HEAD_DIM]
        s = lax.dot_general(qh, kk, (((1,), (1,)), ((), ())), preferred_element_type=F32)
        s = jnp.where(mask, s, NEG_BIG)
        sink = sink_ref[h]
        m = jnp.maximum(jnp.max(s, axis=-1, keepdims=True), sink)
        p = jnp.exp(s - m)
        denom = jnp.sum(p, axis=-1, keepdims=True) + jnp.exp(sink - m)
        o = jnp.dot(p.astype(BF16), vv, preferred_element_type=F32) / denom
        o_ref[0, :, h * HEAD_DIM:(h + 1) * HEAD_DIM] = o.astype(o_ref.dtype)


def _attention(q, kv, sink, cos_t, sin_t, seq_len):
    B, Lp, _ = q.shape
    nb = Lp // BLOCK
    assert nb >= 3
    kvspec = lambda d: pl.BlockSpec((1, BLOCK, 2 * KV_WIDTH), lambda b, i: (b, (i + nb + d) % nb, 0))
    tspec = lambda d: pl.BlockSpec((BLOCK, LANES), lambda b, i: ((i + nb + d) % nb, 0))
    return pl.pallas_call(
        functools.partial(_attn_kernel, seq_len=seq_len, padded_len=Lp),
        grid=(B, nb),
        in_specs=[pl.BlockSpec(memory_space=pltpu.SMEM),
                  pl.BlockSpec((1, BLOCK, ATTN_WIDTH), lambda b, i: (b, i, 0)),
                  kvspec(-1), kvspec(0), kvspec(1),
                  tspec(0), tspec(0), tspec(-1), tspec(-1), tspec(1), tspec(1)],
        out_specs=pl.BlockSpec((1, BLOCK, ATTN_WIDTH), lambda b, i: (b, i, 0)),
        out_shape=jax.ShapeDtypeStruct((B, Lp, ATTN_WIDTH), BF16),
        compiler_params=_cparams(("parallel", "parallel")),
        name="window_attention",
    )(sink, q, kv, kv, kv, cos_t, sin_t, cos_t, sin_t, cos_t, sin_t)


def _shortconv_kernel(uv_ref, u1_ref, u2_ref, wv_ref, w1_ref, w2_ref, bv_ref, b1_ref, b2_ref,
                      v_ref, vb_ref, x1_ref, x2_ref):
    def conv(u_ref, w_ref, b_ref):
        u = u_ref[0]
        n = u.shape[0]
        prev = pltpu.roll(u, 1, 0)
        nxt = pltpu.roll(u, n - 1, 0)
        return w_ref[0:1, :] * prev + w_ref[1:2, :] * u + w_ref[2:3, :] * nxt + b_ref[...]
    v = conv(uv_ref, wv_ref, bv_ref)
    v_ref[0] = v
    vb_ref[0] = v.astype(BF16)
    x1_ref[0] = conv(u1_ref, w1_ref, b1_ref)
    x2_ref[0] = conv(u2_ref, w2_ref, b2_ref)


def _shortconv(hy, conv_w, conv_b):
    B, Lp, _ = hy.shape
    tc = 128
    nct = HY_WIDTH // tc
    uspec = lambda part: pl.BlockSpec((1, Lp, tc), lambda b, j: (b, 0, part * nct + j))
    wspec = lambda part: pl.BlockSpec((3, tc), lambda b, j: (0, part * nct + j))
    bspec = lambda part: pl.BlockSpec((1, tc), lambda b, j: (0, part * nct + j))
    ospec = pl.BlockSpec((1, Lp, tc), lambda b, j: (b, 0, j))
    f32o = jax.ShapeDtypeStruct((B, Lp, HY_WIDTH), F32)
    return pl.pallas_call(
        _shortconv_kernel,
        grid=(B, nct),
        in_specs=[uspec(0), uspec(1), uspec(2), wspec(0), wspec(1), wspec(2), bspec(0), bspec(1), bspec(2)],
        out_specs=[ospec, ospec, ospec, ospec],
        out_shape=[f32o, jax.ShapeDtypeStruct((B, Lp, HY_WIDTH), BF16), f32o, f32o],
        compiler_params=_cparams(("parallel", "parallel")),
        name="hyena_shortconv",
    )(hy, hy, hy, conv_w, conv_w, conv_w, conv_b, conv_b, conv_b)


def _filter_kernel(z_ref, w1_ref, b1_ref, w2_ref, b2_ref, w3_ref, b3_ref, fr_ref, wo_ref, dl_ref,
                   e_ref, o_ref, *, seq_len, padded_len):
    hp = lax.Precision.HIGHEST
    i = pl.program_id(0)
    tr = z_ref.shape[0]
    z = z_ref[...]
    fr = fr_ref[...]
    a = jnp.sin(fr * (jnp.dot(z, w1_ref[...], precision=hp, preferred_element_type=F32) + b1_ref[...]))
    a = jnp.sin(fr * (jnp.dot(a, w2_ref[...], precision=hp, preferred_element_type=F32) + b2_ref[...]))
    a = jnp.sin(fr * (jnp.dot(a, w3_ref[...], precision=hp, preferred_element_type=F32) + b3_ref[...]))
    h = jnp.dot(a, wo_ref[...], precision=hp, preferred_element_type=F32)
    t = z[:, 0:1]
    decay = jnp.exp(-t * dl_ref[...])
    decay = jnp.concatenate([decay] * HY_ORDER, axis=-1)
    ow = HY_ORDER * HY_WIDTH
    hf = h[:, :ow] * decay
    hb = h[:, ow:] * decay
    lag = (i * tr + lax.broadcasted_iota(jnp.int32, (tr, 1), 0) + N_META) % padded_len
    valid = lag < seq_len
    zero_lag = lag == 0
    e_ref[...] = jnp.where(valid, hf + hb, 0.0)
    o_ref[...] = jnp.where(valid, jnp.where(zero_lag, hf + hb, hf - hb), 0.0)


def _filters(L, Lp, w1, b1, w2, b2, w3, b3, freq, wout):
    r = jnp.arange(Lp, dtype=jnp.int32)
    lag = ((r + N_META) % Lp).astype(F32)
    t = lag / max(L - 1, 1)
    f = jnp.linspace(1e-4, FILT_BANDS - 1, FILT_BANDS, dtype=F32)
    ang = (2.0 * math.pi / L) * lag[:, None] * f[None]
    z = jnp.concatenate([t[:, None], jnp.cos(ang), -jnp.sin(ang)], axis=-1)
    z = jnp.pad(z, ((0, 0), (0, LANES - FILT_EMB)))
    w1p = jnp.pad(w1, ((0, LANES - FILT_EMB), (0, 0)))
    max_decay = math.log(DECAY_TARGET) / FAST_DECAY_PCT
    min_decay = math.log(DECAY_TARGET) / SLOW_DECAY_PCT
    deltas = jnp.abs(jnp.linspace(min_decay, max_decay, HY_WIDTH, dtype=F32))[None]
    tr = _pick_tile(Lp, 512, 8)
    full = lambda a: pl.BlockSpec(a.shape, lambda i: (0,) * a.ndim)
    args = [w1p, b1[None], w2, b2[None], w3, b3[None], freq[None], wout, deltas]
    ow = HY_ORDER * HY_WIDTH
    return pl.pallas_call(
        functools.partial(_filter_kernel, seq_len=L, padded_len=Lp),
        grid=(Lp // tr,),
        in_specs=[pl.BlockSpec((tr, LANES), lambda i: (i, 0))] + [full(a) for a in args],
        out_specs=[pl.BlockSpec((tr, ow), lambda i: (i, 0))] * 2,
        out_shape=[jax.ShapeDtypeStruct((Lp, ow), F32)] * 2,
        compiler_params=_cparams(("parallel",)),
        name="hyena_filter_mlp",
    )(z, *args)


def _dft_mats(L, Lp):
    mh = Lp
    n = 2 * mh
    r = jnp.arange(Lp, dtype=jnp.int32)
    t = (r + N_META) % Lp
    valid = (t < L)[None, :]
    f = jnp.arange(mh, dtype=jnp.int32)[:, None]
    ang = ((f * t[None, :]) % n).astype(F32) * (2.0 * math.pi / n)
    c = jnp.cos(ang)
    s = jnp.sin(ang)
    nyq = jnp.where(t % 2 == 0, 1.0, -1.0).astype(F32)[None, :]
    fc = jnp.where(valid, c, 0.0)
    fs = jnp.where(valid, jnp.where(f == 0, nyq, -s), 0.0)
    wgt = jnp.where(f == 0, 1.0 / n, 2.0 / n).astype(F32)
    gc = (fc * wgt).T
    gs = (fs * wgt).T
    return fc.astype(BF16), fs.astype(BF16), gc.astype(BF16), gs.astype(BF16)


def _dft_plain_kernel(fc_ref, fs_ref, z_ref, re_ref, im_ref):
    z = z_ref[0]
    re_ref[0] = jnp.dot(fc_ref[...], z, preferred_element_type=F32)
    im_ref[0] = jnp.dot(fs_ref[...], z, preferred_element_type=F32)


def _dft_plain(fc, fs, z):
    nbatch, Lp, width = z.shape
    mh = fc.shape[0]
    tm = _pick_tile(mh, 512, 16)
    tn = 512
    ospec = pl.BlockSpec((1, tm, tn), lambda i, b, j: (b, i, j))
    osh = jax.ShapeDtypeStruct((nbatch, mh, width), F32)
    return pl.pallas_call(
        _dft_plain_kernel,
        grid=(mh // tm, nbatch, width // tn),
        in_specs=[pl.BlockSpec((tm, Lp), lambda i, b, j: (i, 0)), pl.BlockSpec((tm, Lp), lambda i, b, j: (i, 0)),
                  pl.BlockSpec((1, Lp, tn), lambda i, b, j: (b, 0, j))],
        out_specs=[ospec, ospec],
        out_shape=[osh, osh],
        compiler_params=_cparams(("parallel", "parallel", "parallel")),
        name="hyena_filter_dft",
    )(fc, fs, z)


def _dft_mul_kernel(fc_ref, fs_ref, z_ref, kr_ref, ki_ref, re_ref, im_ref):
    i = pl.program_id(0)
    z = z_ref[0]
    re = jnp.dot(fc_ref[...], z, preferred_element_type=F32)
    im = jnp.dot(fs_ref[...], z, preferred_element_type=F32)
    kr = kr_ref[...]
    ki = ki_ref[...]
    row0 = (i * re.shape[0] + lax.broadcasted_iota(jnp.int32, (re.shape[0], 1), 0)) == 0
    ki_x = jnp.where(row0, 0.0, ki)
    kd = jnp.where(row0, ki, kr)
    re_ref[0] = (re * kr - im * ki_x).astype(BF16)
    im_ref[0] = (re * ki_x + im * kd).astype(BF16)


def _dft_mul(fc, fs, zb, kr, ki, order):
    B, Lp, width = zb.shape
    mh = fc.shape[0]
    tm = _pick_tile(mh, 512, 16)
    ospec = pl.BlockSpec((1, tm, width), lambda i, b: (b, i, 0))
    osh = jax.ShapeDtypeStruct((B, mh, width), BF16)
    return pl.pallas_call(
        _dft_mul_kernel,
        grid=(mh // tm, B),
        in_specs=[pl.BlockSpec((tm, Lp), lambda i, b: (i, 0)), pl.BlockSpec((tm, Lp), lambda i, b: (i, 0)),
                  pl.BlockSpec((1, Lp, width), lambda i, b: (b, 0, 0)),
                  pl.BlockSpec((tm, width), lambda i, b: (i, order)),
                  pl.BlockSpec((tm, width), lambda i, b: (i, order))],
        out_specs=[ospec, ospec],
        out_shape=[osh, osh],
        compiler_params=_cparams(("parallel", "parallel")),
        name="hyena_dft_mul",
    )(fc, fs, zb, kr, ki)


def _idft_gate_kernel(gc_ref, gs_ref, re_ref, im_ref, gate_ref, z_ref, sk_ref, zo_ref, zb_ref):
    conv = jnp.dot(gc_ref[...], re_ref[0], preferred_element_type=F32)
    conv += jnp.dot(gs_ref[...], im_ref[0], preferred_element_type=F32)
    zn = gate_ref[0] * (conv + sk_ref[...] * z_ref[0])
    zo_ref[0] = zn
    zb_ref[0] = zn.astype(BF16)


def _idft_gate(gc, gs, sre, sim, gate, z, skip, order):
    B, mh, width = sre.shape
    Lp = gc.shape[0]
    tm = _pick_tile(Lp, 512, 16)
    rspec = pl.BlockSpec((1, tm, width), lambda i, b: (b, i, 0))
    sspec = pl.BlockSpec((1, mh, width), lambda i, b: (b, 0, 0))
    gspec = pl.BlockSpec((tm, mh), lambda i, b: (i, 0))
    return pl.pallas_call(
        _idft_gate_kernel,
        grid=(Lp // tm, B),
        in_specs=[gspec, gspec, sspec, sspec, rspec, rspec, pl.BlockSpec((1, width), lambda i, b: (0, 0))],
        out_specs=[rspec, rspec],
        out_shape=[jax.ShapeDtypeStruct((B, Lp, width), F32), jax.ShapeDtypeStruct((B, Lp, width), BF16)],
        compiler_params=_cparams(("parallel", "parallel")),
        name="hyena_idft_gate",
    )(gc, gs, sre, sim, gate, z, skip[order][None])


def _hyena(hy, L, conv_w, conv_b, fw1, fb1, fw2, fb2, fw3, fb3, freq, wout, skip):
    B, Lp, _ = hy.shape
    v, vb, x1, x2 = _shortconv(hy, conv_w, conv_b[None])
    fc, fs, gc, gs = _dft_mats(L, Lp)
    e, o = _filters(L, Lp, fw1, fb1, fw2, fb2, fw3, fb3, freq, wout)
    eo = jnp.stack([e, o])
    hi = eo.astype(BF16)
    lo = (eo - hi.astype(F32)).astype(BF16)
    sre, sim = _dft_plain(fc, fs, jnp.concatenate([hi, lo], axis=0))
    sre = sre[:2] + sre[2:]
    sim = sim[:2] + sim[2:]
    row0 = (jnp.arange(Lp) == 0)[:, None]
    kr = sre[0]
    ki = jnp.where(row0, sim[0], sim[1])
    z, zb = v, vb
    for order, gate in enumerate((x1, x2)):
        pre, pim = _dft_mul(fc, fs, zb, kr, ki, order)
        z, zb = _idft_gate(gc, gs, pre, pim, gate, z, skip, order)
    return zb


def _merge_kernel(a_ref, ho_ref, gate_ref, h_ref, wa_ref, wh_ref, wo_ref, gf_ref, wr_ref,
                  hn_ref, hnb_ref, aff_ref):
    ga = jax.nn.sigmoid(gate_ref[:, :D_MODEL])
    gh = jax.nn.sigmoid(gate_ref[:, D_MODEL:])
    merged = ga * jnp.dot(a_ref[...], wa_ref[...], preferred_element_type=F32)
    merged += gh * jnp.dot(ho_ref[...], wh_ref[...], preferred_element_type=F32)
    h = h_ref[...] + jnp.dot(merged.astype(BF16), wo_ref[...], preferred_element_type=F32)
    hn_ref[...] = h
    hn = _rms(h, gf_ref[...])
    hnb_ref[...] = hn.astype(BF16)
    logits = jnp.dot(hn, wr_ref[...], precision=lax.Precision.HIGHEST, preferred_element_type=F32)
    lane = lax.broadcasted_iota(jnp.int32, logits.shape, 1)
    logits = jnp.where(lane < N_EXPERTS, logits, NEG_BIG)
    m = jnp.max(logits, axis=-1, keepdims=True)
    p = jnp.exp(logits - m)
    aff_ref[...] = p / jnp.sum(p, axis=-1, keepdims=True)


def _merge(a2, ho2, gates, h2, wa, wh, wo, gf, wr):
    rows = h2.shape[0]
    tm = _pick_tile(rows, 256, 128)
    row = lambda w_: pl.BlockSpec((tm, w_), lambda i: (i, 0))
    full = lambda a: pl.BlockSpec(a.shape, lambda i: (0,) * a.ndim)
    return pl.pallas_call(
        _merge_kernel,
        grid=(rows // tm,),
        in_specs=[row(ATTN_WIDTH), row(HY_WIDTH), row(2 * D_MODEL), row(D_MODEL),
                  full(wa), full(wh), full(wo), full(gf), full(wr)],
        out_specs=[row(D_MODEL), row(D_MODEL), row(ROUTER_PAD)],
        out_shape=[jax.ShapeDtypeStruct((rows, D_MODEL), F32), jax.ShapeDtypeStruct((rows, D_MODEL), BF16),
                   jax.ShapeDtypeStruct((rows, ROUTER_PAD), F32)],
        compiler_params=_cparams(("parallel",)),
        name="merge_router",
    )(a2, ho2, gates, h2, wa, wh, wo, gf, wr)


def _ffn_kernel(x_ref, g_ref, wg_ref, wu_ref, wd_ref, y_ref):
    x = x_ref[0]
    gt = jnp.dot(x, wg_ref[0], preferred_element_type=F32)
    up = jnp.dot(x, wu_ref[0], preferred_element_type=F32)
    hid = (gt * jax.nn.sigmoid(gt) * up).astype(BF16)
    y_ref[0] = jnp.dot(hid, wd_ref[0], preferred_element_type=F32) * g_ref[0]


def _ffn(xe, ge, wg, wu, wd, tm):
    E, cp, _ = xe.shape
    return pl.pallas_call(
        _ffn_kernel,
        grid=(E, cp // tm),
        in_specs=[pl.BlockSpec((1, tm, D_MODEL), lambda e, i: (e, i, 0)),
                  pl.BlockSpec((1, tm, 1), lambda e, i: (e, i, 0)),
                  pl.BlockSpec((1, D_MODEL, D_EXPERT), lambda e, i: (e, 0, 0)),
                  pl.BlockSpec((1, D_MODEL, D_EXPERT), lambda e, i: (e, 0, 0)),
                  pl.BlockSpec((1, D_EXPERT, D_MODEL), lambda e, i: (e, 0, 0))],
        out_specs=pl.BlockSpec((1, tm, D_MODEL), lambda e, i: (e, i, 0)),
        out_shape=jax.ShapeDtypeStruct((E, cp, D_MODEL), F32),
        compiler_params=_cparams(("parallel", "parallel")),
        name="expert_ffn",
    )(xe, ge, wg, wu, wd)


def _final_kernel(h_ref, m_ref, g_ref, y_ref):
    y_ref[0] = _rms(h_ref[0] + m_ref[0], g_ref[...])


def _final(h3, moe3, g, S):
    B, Lp, _ = h3.shape
    tm = _pick_tile(S, 512, 8)
    spec = pl.BlockSpec((1, tm, D_MODEL), lambda b, i: (b, i, 0))
    return pl.pallas_call(
        _final_kernel,
        grid=(B, S // tm),
        in_specs=[spec, spec, pl.BlockSpec((1, D_MODEL), lambda b, i: (0, 0))],
        out_specs=spec,
        out_shape=jax.ShapeDtypeStruct((B, S, D_MODEL), F32),
        compiler_params=_cparams(("parallel", "parallel")),
        name="final_norm",
    )(h3, moe3, g)


def _rope_tables(Lp):
    half = HEAD_DIM // 2
    inv = 1.0 / (ROPE_THETA ** (jnp.arange(half, dtype=F32) / half))
    pos = ((jnp.arange(Lp, dtype=jnp.int32) + N_META) % Lp).astype(F32)
    ang = pos[:, None] * inv[None]
    cos = jnp.cos(ang)
    sin = jnp.sin(ang)
    reps = LANES // HEAD_DIM
    cos_t = jnp.concatenate([cos, cos] * reps, axis=-1)
    sin_t = jnp.concatenate([-sin, sin] * reps, axis=-1)
    return cos_t, sin_t


def _mixer(x, p):
    B, S, _ = x.shape
    L = S + N_META
    Lp = (L // BLOCK + 1) * BLOCK
    meta = jnp.broadcast_to(p["meta_tokens"][None], (B, N_META, D_MODEL))
    h = jnp.concatenate([x, jnp.zeros((B, Lp - L, D_MODEL), F32), meta], axis=1)
    h2 = h.reshape(B * Lp, D_MODEL)
    q, kv, hy, gates = _inproj(h2, p["norm_mix_g"], p["w_in"])
    cos_t, sin_t = _rope_tables(Lp)
    a = _attention(q.reshape(B, Lp, -1), kv.reshape(B, Lp, -1), p["attn_sink"], cos_t, sin_t, L)
    ho = _hyena(hy.reshape(B, Lp, -1), L, p["hy_conv_w"], p["hy_conv_b"], p["hy_filt_w1"], p["hy_filt_b1"],
                p["hy_filt_w2"], p["hy_filt_b2"], p["hy_filt_w3"], p["hy_filt_b3"], p["hy_filt_freq"],
                p["hy_filt_wout"], p["hy_skip"])
    hnew, hnb, aff = _merge(a.reshape(B * Lp, -1), ho.reshape(B * Lp, -1), gates, h2,
                            p["w_attn_up"], p["w_hyena_up"], p["w_out"], p["norm_ffn_g"], p["w_router"])
    return hnew, hnb, aff, (B, S, L, Lp)


def _route(aff, dims):
    B, S, L, Lp = dims
    n = B * L
    cap = max(1, (CAPACITY_FACTOR * n) // N_EXPERTS)
    r = jnp.arange(B * Lp, dtype=jnp.int32) % Lp
    valid = (r < S) | (r >= Lp - N_META)
    a = jnp.where(valid[:, None], aff[:, :N_EXPERTS], -1.0)
    g, idx = lax.top_k(a.T, cap)
    return g, idx, cap


def kernel(x_prompt, x_sample, meta_tokens, norm_mix_g, w_in, attn_sink, hy_conv_w, hy_conv_b, hy_filt_w1, hy_filt_b1, hy_filt_w2, hy_filt_b2, hy_filt_w3, hy_filt_b3, hy_filt_freq, hy_filt_wout, hy_skip, w_attn_up, w_hyena_up, w_out, norm_ffn_g, w_router, w_exp_gate, w_exp_up, w_exp_down, norm_final_g):
    p = dict(
        meta_tokens=meta_tokens, norm_mix_g=norm_mix_g[0][None], w_in=w_in[0].astype(BF16), attn_sink=attn_sink[0],
        hy_conv_w=hy_conv_w[0], hy_conv_b=hy_conv_b[0], hy_filt_w1=hy_filt_w1[0], hy_filt_b1=hy_filt_b1[0],
        hy_filt_w2=hy_filt_w2[0], hy_filt_b2=hy_filt_b2[0], hy_filt_w3=hy_filt_w3[0], hy_filt_b3=hy_filt_b3[0],
        hy_filt_freq=hy_filt_freq[0], hy_filt_wout=hy_filt_wout[0], hy_skip=hy_skip[0],
        w_attn_up=w_attn_up[0].astype(BF16), w_hyena_up=w_hyena_up[0].astype(BF16), w_out=w_out[0].astype(BF16),
        norm_ffn_g=norm_ffn_g[0][None],
        w_router=jnp.pad(w_router[0], ((0, 0), (0, ROUTER_PAD - N_EXPERTS))),
    )
    groups = [_mixer(x, p) for x in (x_prompt, x_sample)]
    routes = [_route(aff, dims) for (_, _, aff, dims) in groups]
    ctot = sum(cap for (_, _, cap) in routes)
    tm = 256
    cpad = -(-ctot // tm) * tm
    xe = jnp.concatenate([hnb[idx] for (_, hnb, _, _), (_, idx, _) in zip(groups, routes)], axis=1)
    ge = jnp.concatenate([g for (g, _, _) in routes], axis=1)
    xe = jnp.pad(xe, ((0, 0), (0, cpad - ctot), (0, 0)))
    ge = jnp.pad(ge, ((0, 0), (0, cpad - ctot)))[..., None]
    ye = _ffn(xe, ge, w_exp_gate[0].astype(BF16), w_exp_up[0].astype(BF16), w_exp_down[0].astype(BF16), tm)
    outs = []
    off = 0
    for (hnew, _, _, dims), (_, idx, cap) in zip(groups, routes):
        B, S, L, Lp = dims
        moe = jnp.zeros((B * Lp, D_MODEL), F32).at[idx.reshape(-1)].add(ye[:, off:off + cap].reshape(-1, D_MODEL))
        off += cap
        outs.append(_final(hnew.reshape(B, Lp, D_MODEL), moe.reshape(B, Lp, D_MODEL), norm_final_g[None], S))
    return tuple(outs)
```

```python
import functools
import math

import jax
import jax.numpy as jnp
from jax import lax
from jax.experimental import pallas as pl
from jax.experimental.pallas import tpu as pltpu

F32 = jnp.float32
BF16 = jnp.bfloat16

D_MODEL = 1024
N_META = 16
HEAD_DIM = 64
N_HEADS = 8
N_KV = 2
GROUP = N_HEADS // N_KV
ATTN_WIDTH = N_HEADS * HEAD_DIM
KV_WIDTH = N_KV * HEAD_DIM
WINDOW = 128
BLOCK = 128
ROPE_THETA = 10000.0
HY_WIDTH = D_MODEL // 2
HY_ORDER = 2
HY_PROJ = (HY_ORDER + 1) * HY_WIDTH
FILT_EMB = 33
FILT_BANDS = (FILT_EMB - 1) // 2
FILT_HIDDEN = 64
DECAY_TARGET = 1e-2
FAST_DECAY_PCT = 0.3
SLOW_DECAY_PCT = 1.5
N_EXPERTS = 16
CAPACITY_FACTOR = 2
D_EXPERT = 2048
EPS = 1e-6
Q0, K0, V0, HY0, GA0, GH0 = 0, ATTN_WIDTH, ATTN_WIDTH + KV_WIDTH, ATTN_WIDTH + 2 * KV_WIDTH, \
    ATTN_WIDTH + 2 * KV_WIDTH + HY_PROJ, ATTN_WIDTH + 2 * KV_WIDTH + HY_PROJ + D_MODEL
IN_COLS = GH0 + D_MODEL

LANES = 128
ROUTER_PAD = LANES
VMEM_LIMIT = 56 * 1024 * 1024
NEG_BIG = -1e30


def _cparams(sem):
    return pltpu.CompilerParams(dimension_semantics=sem, vmem_limit_bytes=VMEM_LIMIT)


def _pick_tile(n, target, mult):
    best = None
    for t in range(mult, min(n, target) + 1, mult):
        if n % t == 0:
            best = t
    assert best is not None, (n, target, mult)
    return best


def _rms(x, g):
    return x * lax.rsqrt(jnp.mean(x * x, axis=-1, keepdims=True) + EPS) * g


def _inproj_kernel(x_ref, g_ref, w_ref, q_ref, kv_ref, hy_ref, gate_ref):
    xn = _rms(x_ref[...], g_ref[...]).astype(BF16)
    q_ref[...] = jnp.dot(xn, w_ref[:, Q0:K0], preferred_element_type=F32)
    kv_ref[...] = jnp.dot(xn, w_ref[:, K0:HY0], preferred_element_type=F32)
    hy_ref[...] = jnp.dot(xn, w_ref[:, HY0:GA0], preferred_element_type=F32)
    gate_ref[...] = jnp.dot(xn, w_ref[:, GA0:IN_COLS], preferred_element_type=F32)


def _inproj(h2, g, w):
    rows = h2.shape[0]
    tm = _pick_tile(rows, 256, 128)
    row = lambda w_: pl.BlockSpec((tm, w_), lambda i: (i, 0))
    return pl.pallas_call(
        _inproj_kernel,
        grid=(rows // tm,),
        in_specs=[row(D_MODEL), pl.BlockSpec((1, D_MODEL), lambda i: (0, 0)),
                  pl.BlockSpec((D_MODEL, IN_COLS), lambda i: (0, 0))],
        out_specs=[row(ATTN_WIDTH), row(2 * KV_WIDTH), row(HY_PROJ), row(2 * D_MODEL)],
        out_shape=[jax.ShapeDtypeStruct((rows, ATTN_WIDTH), F32),
                   jax.ShapeDtypeStruct((rows, 2 * KV_WIDTH), F32),
                   jax.ShapeDtypeStruct((rows, HY_PROJ), F32),
                   jax.ShapeDtypeStruct((rows, 2 * D_MODEL), F32)],
        compiler_params=_cparams(("parallel",)),
        name="inproj",
    )(h2, g, w)


def _rope(x, cos, sin_signed):
    width = x.shape[-1]
    reps = width // LANES
    if reps > 1:
        cos = jnp.concatenate([cos] * reps, axis=-1)
        sin_signed = jnp.concatenate([sin_signed] * reps, axis=-1)
    lane = lax.broadcasted_iota(jnp.int32, x.shape, 1)
    first_half = (lane % HEAD_DIM) < (HEAD_DIM // 2)
    rot = jnp.where(first_half, pltpu.roll(x, width - HEAD_DIM // 2, 1), pltpu.roll(x, HEAD_DIM // 2, 1))
    return x * cos + rot * sin_signed


def _attn_kernel(sink_ref, q_ref, kvp_ref, kvc_ref, kvn_ref, cq_ref, sq_ref, cp_ref, sp_ref, cn_ref, sn_ref,
                 o_ref, *, seq_len, padded_len):
    i = pl.program_id(1)
    nb = pl.num_programs(1)
    q = (_rope(q_ref[0], cq_ref[...], sq_ref[...]) * (HEAD_DIM ** -0.5)).astype(BF16)
    k = jnp.concatenate([
        _rope(kvp_ref[0, :, :KV_WIDTH], cp_ref[...], sp_ref[...]),
        _rope(kvc_ref[0, :, :KV_WIDTH], cq_ref[...], sq_ref[...]),
        _rope(kvn_ref[0, :, :KV_WIDTH], cn_ref[...], sn_ref[...])], axis=0).astype(BF16)
    v = jnp.concatenate([kvp_ref[0, :, KV_WIDTH:], kvc_ref[0, :, KV_WIDTH:], kvn_ref[0, :, KV_WIDTH:]],
                        axis=0).astype(BF16)
    rq = i * BLOCK + lax.broadcasted_iota(jnp.int32, (BLOCK, 1), 0)
    pos_q = (rq + N_META) % padded_len
    c = lax.broadcasted_iota(jnp.int32, (1, 3 * BLOCK), 1)
    blk = (i + nb - 1 + c // BLOCK) % nb
    pos_k = (blk * BLOCK + c % BLOCK + N_META) % padded_len
    mask = (jnp.abs(pos_k - pos_q) <= WINDOW) & (pos_k < seq_len)
    for h in range(N_HEADS):
        kh = h // GROUP
        qh = q[:, h * HEAD_DIM:(h + 1) * HEAD_DIM]
        kk = k[:, kh * HEAD_DIM:(kh + 1) * HEAD_DIM]
        vv = v[:, kh * HEAD_DIM:(kh + 1) * HEAD_DIM]
        s = lax.dot_general(qh, kk, (((1,), (1,)), ((), ())), preferred_element_type=F32)
        s = jnp.where(mask, s, NEG_BIG)
        sink = sink_ref[h]
        m = jnp.maximum(jnp.max(s, axis=-1, keepdims=True), sink)
        p = jnp.exp(s - m)
        denom = jnp.sum(p, axis=-1, keepdims=True) + jnp.exp(sink - m)
        o = jnp.dot(p.astype(BF16), vv, preferred_element_type=F32) / denom
        o_ref[0, :, h * HEAD_DIM:(h + 1) * HEAD_DIM] = o.astype(o_ref.dtype)


def _attention(q, kv, sink, cos_t, sin_t, seq_len):
    B, Lp, _ = q.shape
    nb = Lp // BLOCK
    assert nb >= 3
    kvspec = lambda d: pl.BlockSpec((1, BLOCK, 2 * KV_WIDTH), lambda b, i: (b, (i + nb + d) % nb, 0))
    tspec = lambda d: pl.BlockSpec((BLOCK, LANES), lambda b, i: ((i + nb + d) % nb, 0))
    return pl.pallas_call(
        functools.partial(_attn_kernel, seq_len=seq_len, padded_len=Lp),
        grid=(B, nb),
        in_specs=[pl.BlockSpec(memory_space=pltpu.SMEM),
                  pl.BlockSpec((1, BLOCK, ATTN_WIDTH), lambda b, i: (b, i, 0)),
                  kvspec(-1), kvspec(0), kvspec(1),
                  tspec(0), tspec(0), tspec(-1), tspec(-1), tspec(1), tspec(1)],
        out_specs=pl.BlockSpec((1, BLOCK, ATTN_WIDTH), lambda b, i: (b, i, 0)),
        out_shape=jax.ShapeDtypeStruct((B, Lp, ATTN_WIDTH), BF16),
        compiler_params=_cparams(("parallel", "parallel")),
        name="window_attention",
    )(sink, q, kv, kv, kv, cos_t, sin_t, cos_t, sin_t, cos_t, sin_t)


def _shortconv_kernel(uv_ref, u1_ref, u2_ref, wv_ref, w1_ref, w2_ref, bv_ref, b1_ref, b2_ref,
                      v_ref, vb_ref, x1_ref, x2_ref):
    def conv(u_ref, w_ref, b_ref):
        u = u_ref[0]
        n = u.shape[0]
        prev = pltpu.roll(u, 1, 0)
        nxt = pltpu.roll(u, n - 1, 0)
        return w_ref[0:1, :] * prev + w_ref[1:2, :] * u + w_ref[2:3, :] * nxt + b_ref[...]
    v = conv(uv_ref, wv_ref, bv_ref)
    v_ref[0] = v
    vb_ref[0] = v.astype(BF16)
    x1_ref[0] = conv(u1_ref, w1_ref, b1_ref)
    x2_ref[0] = conv(u2_ref, w2_ref, b2_ref)


def _shortconv(hy, conv_w, conv_b):
    B, Lp, _ = hy.shape
    tc = 128
    nct = HY_WIDTH // tc
    uspec = lambda part: pl.BlockSpec((1, Lp, tc), lambda b, j: (b, 0, part * nct + j))
    wspec = lambda part: pl.BlockSpec((3, tc), lambda b, j: (0, part * nct + j))
    bspec = lambda part: pl.BlockSpec((1, tc), lambda b, j: (0, part * nct + j))
    ospec = pl.BlockSpec((1, Lp, tc), lambda b, j: (b, 0, j))
    f32o = jax.ShapeDtypeStruct((B, Lp, HY_WIDTH), F32)
    return pl.pallas_call(
        _shortconv_kernel,
        grid=(B, nct),
        in_specs=[uspec(0), uspec(1), uspec(2), wspec(0), wspec(1), wspec(2), bspec(0), bspec(1), bspec(2)],
        out_specs=[ospec, ospec, ospec, ospec],
        out_shape=[f32o, jax.ShapeDtypeStruct((B, Lp, HY_WIDTH), BF16), f32o, f32o],
        compiler_params=_cparams(("parallel", "parallel")),
        name="hyena_shortconv",
    )(hy, hy, hy, conv_w, conv_w, conv_w, conv_b, conv_b, conv_b)


def _filter_kernel(z_ref, w1_ref, b1_ref, w2_ref, b2_ref, w3_ref, b3_ref, fr_ref, wo_ref, dl_ref,
                   e_ref, o_ref, *, seq_len, padded_len):
    hp = lax.Precision.HIGHEST
    i = pl.program_id(0)
    tr = z_ref.shape[0]
    z = z_ref[...]
    fr = fr_ref[...]
    a = jnp.sin(fr * (jnp.dot(z, w1_ref[...], precision=hp, preferred_element_type=F32) + b1_ref[...]))
    a = jnp.sin(fr * (jnp.dot(a, w2_ref[...], precision=hp, preferred_element_type=F32) + b2_ref[...]))
    a = jnp.sin(fr * (jnp.dot(a, w3_ref[...], precision=hp, preferred_element_type=F32) + b3_ref[...]))
    h = jnp.dot(a, wo_ref[...], precision=hp, preferred_element_type=F32)
    t = z[:, 0:1]
    decay = jnp.exp(-t * dl_ref[...])
    decay = jnp.concatenate([decay] * HY_ORDER, axis=-1)
    ow = HY_ORDER * HY_WIDTH
    hf = h[:, :ow] * decay
    hb = h[:, ow:] * decay
    lag = (i * tr + lax.broadcasted_iota(jnp.int32, (tr, 1), 0) + N_META) % padded_len
    valid = lag < seq_len
    zero_lag = lag == 0
    e_ref[...] = jnp.where(valid, hf + hb, 0.0)
    o_ref[...] = jnp.where(valid, jnp.where(zero_lag, hf + hb, hf - hb), 0.0)


def _filters(L, Lp, w1, b1, w2, b2, w3, b3, freq, wout):
    r = jnp.arange(Lp, dtype=jnp.int32)
    lag = ((r + N_META) % Lp).astype(F32)
    t = lag / max(L - 1, 1)
    f = jnp.linspace(1e-4, FILT_BANDS - 1, FILT_BANDS, dtype=F32)
    ang = (2.0 * math.pi / L) * lag[:, None] * f[None]
    z = jnp.concatenate([t[:, None], jnp.cos(ang), -jnp.sin(ang)], axis=-1)
    z = jnp.pad(z, ((0, 0), (0, LANES - FILT_EMB)))
    w1p = jnp.pad(w1, ((0, LANES - FILT_EMB), (0, 0)))
    max_decay = math.log(DECAY_TARGET) / FAST_DECAY_PCT
    min_decay = math.log(DECAY_TARGET) / SLOW_DECAY_PCT
    deltas = jnp.abs(jnp.linspace(min_decay, max_decay, HY_WIDTH, dtype=F32))[None]
    tr = _pick_tile(Lp, 512, 8)
    full = lambda a: pl.BlockSpec(a.shape, lambda i: (0,) * a.ndim)
    args = [w1p, b1[None], w2, b2[None], w3, b3[None], freq[None], wout, deltas]
    ow = HY_ORDER * HY_WIDTH
    return pl.pallas_call(
        functools.partial(_filter_kernel, seq_len=L, padded_len=Lp),
        grid=(Lp // tr,),
        in_specs=[pl.BlockSpec((tr, LANES), lambda i: (i, 0))] + [full(a) for a in args],
        out_specs=[pl.BlockSpec((tr, ow), lambda i: (i, 0))] * 2,
        out_shape=[jax.ShapeDtypeStruct((Lp, ow), F32)] * 2,
        compiler_params=_cparams(("parallel",)),
        name="hyena_filter_mlp",
    )(z, *args)


def _dft_mats(L, Lp):
    mh = Lp
    n = 2 * mh
    r = jnp.arange(Lp, dtype=jnp.int32)
    t = (r + N_META) % Lp
    valid = (t < L)[None, :]
    f = jnp.arange(mh, dtype=jnp.int32)[:, None]
    ang = ((f * t[None, :]) % n).astype(F32) * (2.0 * math.pi / n)
    c = jnp.cos(ang)
    s = jnp.sin(ang)
    nyq = jnp.where(t % 2 == 0, 1.0, -1.0).astype(F32)[None, :]
    fc = jnp.where(valid, c, 0.0)
    fs = jnp.where(valid, jnp.where(f == 0, nyq, -s), 0.0)
    wgt = jnp.where(f == 0, 1.0 / n, 2.0 / n).astype(F32)
    gc = (fc * wgt).T
    gs = (fs * wgt).T
    return fc.astype(BF16), fs.astype(BF16), gc.astype(BF16), gs.astype(BF16)


def _dft_plain_kernel(fc_ref, fs_ref, z_ref, re_ref, im_ref):
    z = z_ref[0]
    re_ref[0] = jnp.dot(fc_ref[...], z, preferred_element_type=F32)
    im_ref[0] = jnp.dot(fs_ref[...], z, preferred_element_type=F32)


def _dft_plain(fc, fs, z):
    nbatch, Lp, width = z.shape
    mh = fc.shape[0]
    tm = _pick_tile(mh, 512, 16)
    tn = 512
    ospec = pl.BlockSpec((1, tm, tn), lambda i, b, j: (b, i, j))
    osh = jax.ShapeDtypeStruct((nbatch, mh, width), F32)
    return pl.pallas_call(
        _dft_plain_kernel,
        grid=(mh // tm, nbatch, width // tn),
        in_specs=[pl.BlockSpec((tm, Lp), lambda i, b, j: (i, 0)), pl.BlockSpec((tm, Lp), lambda i, b, j: (i, 0)),
                  pl.BlockSpec((1, Lp, tn), lambda i, b, j: (b, 0, j))],
        out_specs=[ospec, ospec],
        out_shape=[osh, osh],
        compiler_params=_cparams(("parallel", "parallel", "parallel")),
        name="hyena_filter_dft",
    )(fc, fs, z)


def _dft_mul_kernel(fc_ref, fs_ref, z_ref, kr_ref, ki_ref, re_ref, im_ref):
    i = pl.program_id(0)
    z = z_ref[0]
    re = jnp.dot(fc_ref[...], z, preferred_element_type=F32)
    im = jnp.dot(fs_ref[...], z, preferred_element_type=F32)
    kr = kr_ref[...]
    ki = ki_ref[...]
    row0 = (i * re.shape[0] + lax.broadcasted_iota(jnp.int32, (re.shape[0], 1), 0)) == 0
    ki_x = jnp.where(row0, 0.0, ki)
    kd = jnp.where(row0, ki, kr)
    re_ref[0] = (re * kr - im * ki_x).astype(BF16)
    im_ref[0] = (re * ki_x + im * kd).astype(BF16)


def _dft_mul(fc, fs, zb, kr, ki, order):
    B, Lp, width = zb.shape
    mh = fc.shape[0]
    tm = _pick_tile(mh, 512, 16)
    ospec = pl.BlockSpec((1, tm, width), lambda i, b: (b, i, 0))
    osh = jax.ShapeDtypeStruct((B, mh, width), BF16)
    return pl.pallas_call(
        _dft_mul_kernel,
        grid=(mh // tm, B),
        in_specs=[pl.BlockSpec((tm, Lp), lambda i, b: (i, 0)), pl.BlockSpec((tm, Lp), lambda i, b: (i, 0)),
                  pl.BlockSpec((1, Lp, width), lambda i, b: (b, 0, 0)),
                  pl.BlockSpec((tm, width), lambda i, b: (i, order)),
                  pl.BlockSpec((tm, width), lambda i, b: (i, order))],
        out_specs=[ospec, ospec],
        out_shape=[osh, osh],
        compiler_params=_cparams(("parallel", "parallel")),
        name="hyena_dft_mul",
    )(fc, fs, zb, kr, ki)


def _idft_gate_kernel(gc_ref, gs_ref, re_ref, im_ref, gate_ref, z_ref, sk_ref, zo_ref, zb_ref):
    conv = jnp.dot(gc_ref[...], re_ref[0], preferred_element_type=F32)
    conv += jnp.dot(gs_ref[...], im_ref[0], preferred_element_type=F32)
    zn = gate_ref[0] * (conv + sk_ref[...] * z_ref[0])
    zo_ref[0] = zn
    zb_ref[0] = zn.astype(BF16)


def _idft_gate(gc, gs, sre, sim, gate, z, skip, order):
    B, mh, width = sre.shape
    Lp = gc.shape[0]
    tm = _pick_tile(Lp, 512, 16)
    rspec = pl.BlockSpec((1, tm, width), lambda i, b: (b, i, 0))
    sspec = pl.BlockSpec((1, mh, width), lambda i, b: (b, 0, 0))
    gspec = pl.BlockSpec((tm, mh), lambda i, b: (i, 0))
    return pl.pallas_call(
        _idft_gate_kernel,
        grid=(Lp // tm, B),
        in_specs=[gspec, gspec, sspec, sspec, rspec, rspec, pl.BlockSpec((1, width), lambda i, b: (0, 0))],
        out_specs=[rspec, rspec],
        out_shape=[jax.ShapeDtypeStruct((B, Lp, width), F32), jax.ShapeDtypeStruct((B, Lp, width), BF16)],
        compiler_params=_cparams(("parallel", "parallel")),
        name="hyena_idft_gate",
    )(gc, gs, sre, sim, gate, z, skip[order][None])


def _hyena(hy, L, conv_w, conv_b, fw1, fb1, fw2, fb2, fw3, fb3, freq, wout, skip):
    B, Lp, _ = hy.shape
    v, vb, x1, x2 = _shortconv(hy, conv_w, conv_b[None])
    fc, fs, gc, gs = _dft_mats(L, Lp)
    e, o = _filters(L, Lp, fw1, fb1, fw2, fb2, fw3, fb3, freq, wout)
    eo = jnp.stack([e, o])
    hi = eo.astype(BF16)
    lo = (eo - hi.astype(F32)).astype(BF16)
    sre, sim = _dft_plain(fc, fs, jnp.concatenate([hi, lo], axis=0))
    sre = sre[:2] + sre[2:]
    sim = sim[:2] + sim[2:]
    row0 = (jnp.arange(Lp) == 0)[:, None]
    kr = sre[0]
    ki = jnp.where(row0, sim[0], sim[1])
    z, zb = v, vb
    for order, gate in enumerate((x1, x2)):
        pre, pim = _dft_mul(fc, fs, zb, kr, ki, order)
        z, zb = _idft_gate(gc, gs, pre, pim, gate, z, skip, order)
    return zb


EXT_WIDTH = D_MODEL + LANES


def _merge_kernel(a_ref, ho_ref, gate_ref, h_ref, wa_ref, wh_ref, wo_ref, gf_ref, wr_ref, *rest):
    hnew_ref, ext_ref, afft_ref = rest[-3:]
    ga = jax.nn.sigmoid(gate_ref[:, :D_MODEL])
    gh = jax.nn.sigmoid(gate_ref[:, D_MODEL:])
    merged = ga * jnp.dot(a_ref[...], wa_ref[...], preferred_element_type=F32)
    merged += gh * jnp.dot(ho_ref[...], wh_ref[...], preferred_element_type=F32)
    h = h_ref[...] + jnp.dot(merged.astype(BF16), wo_ref[...], preferred_element_type=F32)
    hnew_ref[...] = h
    hn = _rms(h, gf_ref[...])
    logits = jnp.dot(hn, wr_ref[...], precision=lax.Precision.HIGHEST, preferred_element_type=F32)
    lane = lax.broadcasted_iota(jnp.int32, logits.shape, 1)
    logits = jnp.where(lane < N_EXPERTS, logits, NEG_BIG)
    m = jnp.max(logits, axis=-1, keepdims=True)
    p = jnp.exp(logits - m)
    aff = p / jnp.sum(p, axis=-1, keepdims=True)
    ext_ref[:, :D_MODEL] = hn
    ext_ref[:, D_MODEL:] = aff
    afft_ref[...] = aff.T[:N_EXPERTS]


def _merge(a2, ho2, gates, h2, wa, wh, wo, gf, wr, ext_prev, row_base, rows_total):
    rows = h2.shape[0]
    tm = _pick_tile(rows, 256, 128)
    assert row_base % tm == 0
    row = lambda w_: pl.BlockSpec((tm, w_), lambda i: (i, 0))
    full = lambda a: pl.BlockSpec(a.shape, lambda i: (0,) * a.ndim)
    args = [a2, ho2, gates, h2, wa, wh, wo, gf, wr]
    in_specs = [row(ATTN_WIDTH), row(HY_WIDTH), row(2 * D_MODEL), row(D_MODEL),
                full(wa), full(wh), full(wo), full(gf), full(wr)]
    aliases = {}
    if ext_prev is not None:
        args.append(ext_prev)
        in_specs.append(pl.BlockSpec(memory_space=pl.ANY))
        aliases = {len(args) - 1: 1}
    return pl.pallas_call(
        _merge_kernel,
        grid=(rows // tm,),
        in_specs=in_specs,
        out_specs=[row(D_MODEL), pl.BlockSpec((tm, EXT_WIDTH), lambda i: (i + row_base // tm, 0)),
                   pl.BlockSpec((N_EXPERTS, tm), lambda i: (0, i))],
        out_shape=[jax.ShapeDtypeStruct((rows, D_MODEL), F32), jax.ShapeDtypeStruct((rows_total, EXT_WIDTH), F32),
                   jax.ShapeDtypeStruct((N_EXPERTS, rows), F32)],
        input_output_aliases=aliases,
        compiler_params=_cparams(("arbitrary",)),
        name="merge_router",
    )(*args)


ROUTE_CHUNK = 256
SLOT_TILE = 128


def _select_kernel(aff_ref, triu_ref, tril_ref, incl_ref, dval_ref, start_ref, end_ref, bits_ref,
                   *, cap, seq_tokens, padded_len, y_base):
    rows = aff_ref.shape[1]
    ch = triu_ref.shape[0]
    lane = lax.broadcasted_iota(jnp.int32, (1, rows), 1)
    rl = lane % padded_len
    valid = (rl < seq_tokens) | (rl >= padded_len - N_META)
    bits_ref[...] = jnp.where(valid, pltpu.bitcast(aff_ref[...], jnp.int32), -1)

    def count_ge(cand):
        return jnp.sum(jnp.where(bits_ref[...] >= cand, 1.0, 0.0), axis=1, keepdims=True)

    def bisect(k, thr):
        cand = thr | jnp.left_shift(jnp.int32(1), 30 - k)
        return jnp.where(count_ge(cand) >= cap, cand, thr)

    thr = lax.fori_loop(0, 31, bisect, jnp.zeros((N_EXPERTS, 1), jnp.int32))
    need = cap - count_ge(thr + 1)
    triu = triu_ref[...]
    tril = tril_ref[...]

    def chunk(c, carry):
        ceq, csel, crow = carry
        sl = pl.ds(pl.multiple_of(c * ch, ch), ch)
        b = bits_ref[:, sl]
        eq = b == thr
        incl_eq = jnp.dot(jnp.where(eq, 1.0, 0.0).astype(BF16), triu, preferred_element_type=F32) + ceq
        sel = (b > thr) | (eq & (incl_eq <= need))
        self_ = jnp.where(sel, 1.0, 0.0)
        selb = self_.astype(BF16)
        incl = jnp.dot(selb, triu, preferred_element_type=F32) + csel
        kt = jnp.sum(self_, axis=0, keepdims=True)
        kt8 = jnp.broadcast_to(kt, (8, ch)).astype(BF16)
        excl = jnp.dot(kt8, triu, preferred_element_type=F32)[0:1] - kt + crow
        rank = jnp.dot(tril, selb, preferred_element_type=F32)
        incl_ref[:, sl] = jnp.where(sel, incl, -incl - 1.0)
        dval_ref[:, sl] = excl + rank
        start_ref[:, sl] = excl
        end_ref[:, sl] = excl + kt
        return (ceq + jnp.sum(jnp.where(eq, 1.0, 0.0), axis=1, keepdims=True),
                csel + jnp.sum(self_, axis=1, keepdims=True),
                crow + jnp.sum(kt, axis=1, keepdims=True))

    zero = jnp.zeros((N_EXPERTS, 1), F32)
    lax.fori_loop(0, rows // ch, chunk, (zero, zero, jnp.full((1, 1), y_base, F32)))


def _select(afft, cap, seq_tokens, padded_len, y_base):
    rows = afft.shape[1]
    ch = ROUTE_CHUNK
    assert rows % ch == 0
    triu = (jnp.arange(ch)[:, None] <= jnp.arange(ch)[None, :]).astype(BF16)
    tril = (jnp.arange(N_EXPERTS)[:, None] > jnp.arange(N_EXPERTS)[None, :]).astype(BF16)
    full = lambda shape: pl.BlockSpec(shape, lambda i: (0,) * len(shape))
    return pl.pallas_call(
        functools.partial(_select_kernel, cap=cap, seq_tokens=seq_tokens, padded_len=padded_len, y_base=y_base),
        grid=(1,),
        in_specs=[full((N_EXPERTS, rows)), full((ch, ch)), full((N_EXPERTS, N_EXPERTS))],
        out_specs=[full((N_EXPERTS, rows)), full((N_EXPERTS, rows)), full((1, rows)), full((1, rows))],
        out_shape=[jax.ShapeDtypeStruct((N_EXPERTS, rows), F32), jax.ShapeDtypeStruct((N_EXPERTS, rows), F32),
                   jax.ShapeDtypeStruct((1, rows), F32), jax.ShapeDtypeStruct((1, rows), F32)],
        scratch_shapes=[pltpu.VMEM((N_EXPERTS, rows), jnp.int32)],
        compiler_params=_cparams(("arbitrary",)),
        name="expert_select",
    )(afft, triu, tril)


def _lists_kernel(offs_ref, incl_ref, dval_ref, idx_ref, dst_ref, *, cap, nchunks, row_base, dump_base):
    ch = ROUTE_CHUNK
    t = SLOT_TILE
    cp = idx_ref.shape[0]
    slot_all = lax.broadcasted_iota(jnp.int32, (cp, LANES), 0)
    lane_all = lax.broadcasted_iota(jnp.int32, (cp, LANES), 1)
    idx_ref[...] = jnp.full((cp, LANES), row_base, jnp.int32)
    dst_ref[...] = dump_base + jnp.maximum(slot_all - cap, 0) * N_EXPERTS + jnp.minimum(lane_all, N_EXPERTS - 1)
    lane = lax.broadcasted_iota(jnp.int32, (t, LANES), 1)
    tok_lane = lax.broadcasted_iota(jnp.int32, (1, ch), 1)

    def chunk(c, _):
        csl = pl.ds(pl.multiple_of(c * ch, ch), ch)
        tok = (row_base + c * ch + tok_lane).astype(F32)
        for e in range(N_EXPERTS):
            o0 = offs_ref[e * (nchunks + 1) + c]
            o1 = offs_ref[e * (nchunks + 1) + c + 1]
            row_incl = incl_ref[e:e + 1, csl]
            row_dval = dval_ref[e:e + 1, csl]

            def tile(j, _):
                sl = pl.ds(pl.multiple_of(j * t, t), t)
                slot = j * t + lax.broadcasted_iota(jnp.int32, (t, 1), 0)
                hit = row_incl == (slot + 1).astype(F32)
                src = jnp.sum(jnp.where(hit, tok, 0.0), axis=1, keepdims=True).astype(jnp.int32)
                dst = jnp.sum(jnp.where(hit, row_dval, 0.0), axis=1, keepdims=True).astype(jnp.int32)
                m = (slot >= o0) & (slot < o1) & (lane == e)
                idx_ref[sl, :] = jnp.where(m, src, idx_ref[sl, :])
                dst_ref[sl, :] = jnp.where(m, dst, dst_ref[sl, :])
                return 0

            lax.fori_loop(o0 // t, (o1 + t - 1) // t, tile, 0)
        return 0

    lax.fori_loop(0, nchunks, chunk, 0)


def _lists(offs, incl, dval, cap, cp, row_base, dump_base):
    rows = incl.shape[1]
    nchunks = rows // ROUTE_CHUNK
    full = lambda shape: pl.BlockSpec(shape, lambda i, o: (0,) * len(shape))
    return pl.pallas_call(
        functools.partial(_lists_kernel, cap=cap, nchunks=nchunks, row_base=row_base, dump_base=dump_base),
        grid_spec=pltpu.PrefetchScalarGridSpec(
            num_scalar_prefetch=1, grid=(1,),
            in_specs=[full((N_EXPERTS, rows)), full((N_EXPERTS, rows))],
            out_specs=[full((cp, LANES)), full((cp, LANES))]),
        out_shape=[jax.ShapeDtypeStruct((cp, LANES), jnp.int32)] * 2,
        compiler_params=_cparams(("arbitrary",)),
        name="expert_slot_lists",
    )(offs, incl, dval)


FFN_TILE = 256


def _ffn_kernel(idx_hbm, dst_hbm, ext_hbm, wg_ref, wu_ref, wd_ref, y_hbm,
                xbuf, ybuf, idx_sm, dst_sm, gsem, ssem, isem, dsem):
    e = pl.program_id(0)
    i = pl.program_id(1)
    nt = pl.num_programs(1)
    total = pl.num_programs(0) * nt
    s = e * nt + i
    slot = s % 2
    tm = FFN_TILE

    def idx_copy(step, buf):
        return pltpu.make_async_copy(idx_hbm.at[step], idx_sm.at[buf], isem)

    def dst_copy(step):
        return pltpu.make_async_copy(dst_hbm.at[step], dst_sm, dsem)

    def gather_start(buf):
        def body(r, _):
            pltpu.make_async_copy(ext_hbm.at[pl.ds(idx_sm[buf, r], 1)], xbuf.at[buf, pl.ds(r, 1)],
                                  gsem.at[buf]).start()
            return 0
        lax.fori_loop(0, tm, body, 0, unroll=8)

    def gather_wait(buf):
        pltpu.make_async_copy(ext_hbm.at[pl.ds(0, tm)], xbuf.at[buf], gsem.at[buf]).wait()

    def scatter_wait():
        pltpu.make_async_copy(ybuf, y_hbm.at[pl.ds(0, tm)], ssem).wait()

    @pl.when(s == 0)
    def _():
        c = idx_copy(0, 0)
        c.start()
        c.wait()
        gather_start(0)

    dst_copy(s).start()

    @pl.when(s + 1 < total)
    def _():
        c = idx_copy(s + 1, 1 - slot)
        c.start()
        c.wait()
        gather_start(1 - slot)

    gather_wait(slot)
    x = xbuf[slot]
    lane = lax.broadcasted_iota(jnp.int32, (tm, LANES), 1)
    gate = jnp.sum(jnp.where(lane == e, x[:, D_MODEL:], 0.0), axis=1, keepdims=True)
    xb = x[:, :D_MODEL].astype(BF16)
    gt = jnp.dot(xb, wg_ref[0], preferred_element_type=F32)
    up = jnp.dot(xb, wu_ref[0], preferred_element_type=F32)
    hid = (gt * jax.nn.sigmoid(gt) * up).astype(BF16)
    y = jnp.dot(hid, wd_ref[0], preferred_element_type=F32) * gate

    @pl.when(s > 0)
    def _():
        scatter_wait()

    ybuf[...] = y
    dst_copy(s).wait()

    def sbody(r, _):
        pltpu.make_async_copy(ybuf.at[pl.ds(r, 1)], y_hbm.at[pl.ds(dst_sm[r], 1)], ssem).start()
        return 0
    lax.fori_loop(0, tm, sbody, 0, unroll=8)

    @pl.when(s == total - 1)
    def _():
        scatter_wait()


def _ffn(idx, dst, ext, wg, wu, wd, y_rows):
    E = wg.shape[0]
    nt = idx.shape[0] // E
    tm = FFN_TILE
    anyspec = pl.BlockSpec(memory_space=pl.ANY)
    return pl.pallas_call(
        _ffn_kernel,
        grid=(E, nt),
        in_specs=[anyspec, anyspec, anyspec,
                  pl.BlockSpec((1, D_MODEL, D_EXPERT), lambda e, i: (e, 0, 0)),
                  pl.BlockSpec((1, D_MODEL, D_EXPERT), lambda e, i: (e, 0, 0)),
                  pl.BlockSpec((1, D_EXPERT, D_MODEL), lambda e, i: (e, 0, 0))],
        out_specs=anyspec,
        out_shape=jax.ShapeDtypeStruct((y_rows, D_MODEL), F32),
        scratch_shapes=[pltpu.VMEM((2, tm, EXT_WIDTH), F32), pltpu.VMEM((tm, D_MODEL), F32),
                        pltpu.SMEM((2, tm), jnp.int32), pltpu.SMEM((tm,), jnp.int32),
                        pltpu.SemaphoreType.DMA((2,)), pltpu.SemaphoreType.DMA,
                        pltpu.SemaphoreType.DMA, pltpu.SemaphoreType.DMA],
        compiler_params=_cparams(("arbitrary", "arbitrary")),
        name="expert_ffn",
    )(idx, dst, ext, wg, wu, wd)


COMBINE_CHUNK = 128
Y_BLOCK = 256


def _combine_kernel(cg_ref, bi_ref, ci_ref, blk_ref, flag_ref, y_ref, start_ref, end_ref, h_ref, g_ref, o_ref, acc_ref,
                    *, y_limit):
    w = pl.program_id(0)
    flags = flag_ref[w]

    @pl.when((flags & 1) != 0)
    def _():
        acc_ref[...] = jnp.zeros_like(acc_ref)

    @pl.when((flags & 4) != 0)
    def _():
        tc = COMBINE_CHUNK
        st = jnp.broadcast_to(start_ref[...], (tc, tc)).T
        en = jnp.broadcast_to(end_ref[...], (tc, tc)).T
        reps = Y_BLOCK // tc
        st = jnp.concatenate([st] * reps, axis=1)
        en = jnp.concatenate([en] * reps, axis=1)
        jg = (blk_ref[w] * Y_BLOCK + lax.broadcasted_iota(jnp.int32, (1, Y_BLOCK), 1)).astype(F32)
        onehot = jnp.where((jg >= st) & (jg < en), 1.0, 0.0).astype(BF16)
        jrow = blk_ref[w] * Y_BLOCK + lax.broadcasted_iota(jnp.int32, (Y_BLOCK, 1), 0)
        y = jnp.where(jrow < y_limit, y_ref[...], 0.0)
        hi = y.astype(BF16)
        lo = (y - hi.astype(F32)).astype(BF16)
        acc_ref[...] += (jnp.dot(onehot, hi, preferred_element_type=F32)
                         + jnp.dot(onehot, lo, preferred_element_type=F32))

    @pl.when((flags & 2) != 0)
    def _():
        o_ref[0] = _rms(h_ref[...] + acc_ref[...], g_ref[...])


def _combine(work, y, start, end, hnew, g, B, S, y_limit):
    cg, bi, ci, blk, flags = work
    nw = cg.shape[0]
    tc = COMBINE_CHUNK
    assert S % tc == 0 and hnew.shape[0] % tc == 0
    return pl.pallas_call(
        functools.partial(_combine_kernel, y_limit=y_limit),
        grid_spec=pltpu.PrefetchScalarGridSpec(
            num_scalar_prefetch=5, grid=(nw,),
            in_specs=[pl.BlockSpec((Y_BLOCK, D_MODEL), lambda w, cg, bi, ci, blk, fl: (blk[w], 0)),
                      pl.BlockSpec((1, tc), lambda w, cg, bi, ci, blk, fl: (0, cg[w])),
                      pl.BlockSpec((1, tc), lambda w, cg, bi, ci, blk, fl: (0, cg[w])),
                      pl.BlockSpec((tc, D_MODEL), lambda w, cg, bi, ci, blk, fl: (cg[w], 0)),
                      pl.BlockSpec((1, D_MODEL), lambda w, cg, bi, ci, blk, fl: (0, 0))],
            out_specs=pl.BlockSpec((1, tc, D_MODEL), lambda w, cg, bi, ci, blk, fl: (bi[w], ci[w], 0)),
            scratch_shapes=[pltpu.VMEM((tc, D_MODEL), F32)]),
        out_shape=jax.ShapeDtypeStruct((B, S, D_MODEL), F32),
        compiler_params=_cparams(("arbitrary",)),
        name="combine_final_norm",
    )(cg, bi, ci, blk, flags, y, start, end, hnew, g)


def _combine_worklist(start, end, B, S, Lp, n_contrib):
    tc = COMBINE_CHUNK
    cpb = S // tc
    b = jnp.repeat(jnp.arange(B, dtype=jnp.int32), cpb)
    ci = jnp.tile(jnp.arange(cpb, dtype=jnp.int32), B)
    cg = b * (Lp // tc) + ci
    lo = start[0, cg * tc].astype(jnp.int32)
    hi = end[0, cg * tc + tc - 1].astype(jnp.int32)
    nblk = jnp.where(hi > lo, (hi - 1) // Y_BLOCK - lo // Y_BLOCK + 1, 1)
    cum = jnp.cumsum(nblk)
    nchunks = B * cpb
    nw = nchunks + n_contrib // Y_BLOCK + 2
    w = jnp.arange(nw, dtype=jnp.int32)
    valid = w < cum[-1]
    c = jnp.minimum(jnp.searchsorted(cum, w, side="right").astype(jnp.int32), nchunks - 1)
    k = w - (cum[c] - nblk[c])
    k = jnp.where(valid, k, nblk[c] - 1)
    blk = lo[c] // Y_BLOCK + k
    flags = jnp.where(valid, (k == 0) * 1 + (k == nblk[c] - 1) * 2 + 4, 0).astype(jnp.int32)
    return cg[c], b[c], ci[c], blk.astype(jnp.int32), flags


def _rope_tables(Lp):
    half = HEAD_DIM // 2
    inv = 1.0 / (ROPE_THETA ** (jnp.arange(half, dtype=F32) / half))
    pos = ((jnp.arange(Lp, dtype=jnp.int32) + N_META) % Lp).astype(F32)
    ang = pos[:, None] * inv[None]
    cos = jnp.cos(ang)
    sin = jnp.sin(ang)
    reps = LANES // HEAD_DIM
    cos_t = jnp.concatenate([cos, cos] * reps, axis=-1)
    sin_t = jnp.concatenate([-sin, sin] * reps, axis=-1)
    return cos_t, sin_t


def _dims(x):
    B, S, _ = x.shape
    L = S + N_META
    Lp = (L // BLOCK + 1) * BLOCK
    return B, S, L, Lp


def _mixer(x, p, ext_prev, row_base, rows_total):
    B, S, L, Lp = _dims(x)
    meta = jnp.broadcast_to(p["meta_tokens"][None], (B, N_META, D_MODEL))
    h = jnp.concatenate([x, jnp.zeros((B, Lp - L, D_MODEL), F32), meta], axis=1)
    h2 = h.reshape(B * Lp, D_MODEL)
    q, kv, hy, gates = _inproj(h2, p["norm_mix_g"], p["w_in"])
    cos_t, sin_t = _rope_tables(Lp)
    a = _attention(q.reshape(B, Lp, -1), kv.reshape(B, Lp, -1), p["attn_sink"], cos_t, sin_t, L)
    ho = _hyena(hy.reshape(B, Lp, -1), L, p["hy_conv_w"], p["hy_conv_b"], p["hy_filt_w1"], p["hy_filt_b1"],
                p["hy_filt_w2"], p["hy_filt_b2"], p["hy_filt_w3"], p["hy_filt_b3"], p["hy_filt_freq"],
                p["hy_filt_wout"], p["hy_skip"])
    return _merge(a.reshape(B * Lp, -1), ho.reshape(B * Lp, -1), gates, h2,
                  p["w_attn_up"], p["w_hyena_up"], p["w_out"], p["norm_ffn_g"], p["w_router"],
                  ext_prev, row_base, rows_total)


def _route(afft, dims, row_base, y_base, dump_base):
    B, S, L, Lp = dims
    cap = max(1, (CAPACITY_FACTOR * B * L) // N_EXPERTS)
    cp = -(-cap // FFN_TILE) * FFN_TILE
    incl, dval, start, end = _select(afft, cap, S, Lp, y_base)
    last = incl[:, ROUTE_CHUNK - 1::ROUTE_CHUNK]
    counts = jnp.where(last >= 0, last, -last - 1.0).astype(jnp.int32)
    offs = jnp.concatenate([jnp.zeros((N_EXPERTS, 1), jnp.int32), counts], axis=1).reshape(-1)
    idx_t, dst_t = _lists(offs, incl, dval, cap, cp, row_base, dump_base)
    return idx_t[:, :N_EXPERTS].T, dst_t[:, :N_EXPERTS].T, start, end, cap, cp


def kernel(x_prompt, x_sample, meta_tokens, norm_mix_g, w_in, attn_sink, hy_conv_w, hy_conv_b, hy_filt_w1, hy_filt_b1, hy_filt_w2, hy_filt_b2, hy_filt_w3, hy_filt_b3, hy_filt_freq, hy_filt_wout, hy_skip, w_attn_up, w_hyena_up, w_out, norm_ffn_g, w_router, w_exp_gate, w_exp_up, w_exp_down, norm_final_g):
    p = dict(
        meta_tokens=meta_tokens, norm_mix_g=norm_mix_g[0][None], w_in=w_in[0].astype(BF16), attn_sink=attn_sink[0],
        hy_conv_w=hy_conv_w[0], hy_conv_b=hy_conv_b[0], hy_filt_w1=hy_filt_w1[0], hy_filt_b1=hy_filt_b1[0],
        hy_filt_w2=hy_filt_w2[0], hy_filt_b2=hy_filt_b2[0], hy_filt_w3=hy_filt_w3[0], hy_filt_b3=hy_filt_b3[0],
        hy_filt_freq=hy_filt_freq[0], hy_filt_wout=hy_filt_wout[0], hy_skip=hy_skip[0],
        w_attn_up=w_attn_up[0].astype(BF16), w_hyena_up=w_hyena_up[0].astype(BF16), w_out=w_out[0].astype(BF16),
        norm_ffn_g=norm_ffn_g[0][None],
        w_router=jnp.pad(w_router[0], ((0, 0), (0, ROUTER_PAD - N_EXPERTS))),
    )
    xs = (x_prompt, x_sample)
    dims = [_dims(x) for x in xs]
    rows = [B * Lp for (B, S, L, Lp) in dims]
    rows_total = sum(rows)
    caps = [max(1, (CAPACITY_FACTOR * B * L) // N_EXPERTS) for (B, S, L, Lp) in dims]
    cps = [-(-c // FFN_TILE) * FFN_TILE for c in caps]
    y_bases = [0, N_EXPERTS * caps[0]]
    n_contrib = N_EXPERTS * sum(caps)
    dump_bases = [n_contrib, n_contrib + N_EXPERTS * (cps[0] - caps[0])]
    y_rows = n_contrib + N_EXPERTS * sum(cp - c for cp, c in zip(cps, caps))
    y_rows = -(-y_rows // Y_BLOCK) * Y_BLOCK

    ext = jnp.zeros((rows_total, EXT_WIDTH), F32)
    hnews, routes = [], []
    row_base = 0
    for g, x in enumerate(xs):
        hnew, ext, afft = _mixer(x, p, ext, row_base, rows_total)
        hnews.append(hnew)
        routes.append(_route(afft, dims[g], row_base, y_bases[g], dump_bases[g]))
        row_base += rows[g]
    idx = jnp.concatenate([r[0] for r in routes], axis=1).reshape(-1, FFN_TILE)
    dst = jnp.concatenate([r[1] for r in routes], axis=1).reshape(-1, FFN_TILE)
    y = _ffn(idx, dst, ext, w_exp_gate[0].astype(BF16), w_exp_up[0].astype(BF16), w_exp_down[0].astype(BF16), y_rows)
    outs = []
    for g in range(len(xs)):
        B, S, L, Lp = dims[g]
        _, _, start, end, cap, _ = routes[g]
        work = _combine_worklist(start, end, B, S, Lp, N_EXPERTS * cap)
        outs.append(_combine(work, y, start, end, hnews[g], norm_final_g[None], B, S, n_contrib))
    return tuple(outs)
```

```python
import functools
import math

import jax
import jax.numpy as jnp
from jax import lax
from jax.experimental import pallas as pl
from jax.experimental.pallas import tpu as pltpu

F32 = jnp.float32
BF16 = jnp.bfloat16

D_MODEL = 1024
N_META = 16
HEAD_DIM = 64
N_HEADS = 8
N_KV = 2
GROUP = N_HEADS // N_KV
ATTN_WIDTH = N_HEADS * HEAD_DIM
KV_WIDTH = N_KV * HEAD_DIM
WINDOW = 128
BLOCK = 128
ROPE_THETA = 10000.0
HY_WIDTH = D_MODEL // 2
HY_ORDER = 2
HY_PROJ = (HY_ORDER + 1) * HY_WIDTH
FILT_EMB = 33
FILT_BANDS = (FILT_EMB - 1) // 2
FILT_HIDDEN = 64
DECAY_TARGET = 1e-2
FAST_DECAY_PCT = 0.3
SLOW_DECAY_PCT = 1.5
N_EXPERTS = 16
CAPACITY_FACTOR = 2
D_EXPERT = 2048
EPS = 1e-6
Q0, K0, V0, HY0, GA0, GH0 = 0, ATTN_WIDTH, ATTN_WIDTH + KV_WIDTH, ATTN_WIDTH + 2 * KV_WIDTH, \
    ATTN_WIDTH + 2 * KV_WIDTH + HY_PROJ, ATTN_WIDTH + 2 * KV_WIDTH + HY_PROJ + D_MODEL
IN_COLS = GH0 + D_MODEL

LANES = 128
ROUTER_PAD = LANES
VMEM_LIMIT = 56 * 1024 * 1024
NEG_BIG = -1e30


def _cparams(sem):
    return pltpu.CompilerParams(dimension_semantics=sem, vmem_limit_bytes=VMEM_LIMIT)


def _pick_tile(n, target, mult):
    best = None
    for t in range(mult, min(n, target) + 1, mult):
        if n % t == 0:
            best = t
    assert best is not None, (n, target, mult)
    return best


def _rms(x, g):
    return x * lax.rsqrt(jnp.mean(x * x, axis=-1, keepdims=True) + EPS) * g


def _inproj_kernel(x_ref, g_ref, w_ref, q_ref, kv_ref, hy_ref, gate_ref):
    xn = _rms(x_ref[...], g_ref[...]).astype(BF16)
    q_ref[...] = jnp.dot(xn, w_ref[:, Q0:K0], preferred_element_type=F32)
    kv_ref[...] = jnp.dot(xn, w_ref[:, K0:HY0], preferred_element_type=F32)
    hy_ref[...] = jnp.dot(xn, w_ref[:, HY0:GA0], preferred_element_type=F32)
    gate_ref[...] = jnp.dot(xn, w_ref[:, GA0:IN_COLS], preferred_element_type=F32)


def _inproj(h2, g, w):
    rows = h2.shape[0]
    tm = _pick_tile(rows, 256, 128)
    row = lambda w_: pl.BlockSpec((tm, w_), lambda i: (i, 0))
    return pl.pallas_call(
        _inproj_kernel,
        grid=(rows // tm,),
        in_specs=[row(D_MODEL), pl.BlockSpec((1, D_MODEL), lambda i: (0, 0)),
                  pl.BlockSpec((D_MODEL, IN_COLS), lambda i: (0, 0))],
        out_specs=[row(ATTN_WIDTH), row(2 * KV_WIDTH), row(HY_PROJ), row(2 * D_MODEL)],
        out_shape=[jax.ShapeDtypeStruct((rows, ATTN_WIDTH), F32),
                   jax.ShapeDtypeStruct((rows, 2 * KV_WIDTH), F32),
                   jax.ShapeDtypeStruct((rows, HY_PROJ), F32),
                   jax.ShapeDtypeStruct((rows, 2 * D_MODEL), F32)],
        compiler_params=_cparams(("parallel",)),
        name="inproj",
    )(h2, g, w)


def _rope(x, cos, sin_signed):
    width = x.shape[-1]
    reps = width // LANES
    if reps > 1:
        cos = jnp.concatenate([cos] * reps, axis=-1)
        sin_signed = jnp.concatenate([sin_signed] * reps, axis=-1)
    lane = lax.broadcasted_iota(jnp.int32, x.shape, 1)
    first_half = (lane % HEAD_DIM) < (HEAD_DIM // 2)
    rot = jnp.where(first_half, pltpu.roll(x, width - HEAD_DIM // 2, 1), pltpu.roll(x, HEAD_DIM // 2, 1))
    return x * cos + rot * sin_signed


def _attn_kernel(sink_ref, q_ref, kvp_ref, kvc_ref, kvn_ref, cq_ref, sq_ref, cp_ref, sp_ref, cn_ref, sn_ref,
                 o_ref, *, seq_len, padded_len):
    i = pl.program_id(1)
    nb = pl.num_programs(1)
    q = (_rope(q_ref[0], cq_ref[...], sq_ref[...]) * (HEAD_DIM ** -0.5)).astype(BF16)
    k = jnp.concatenate([
        _rope(kvp_ref[0, :, :KV_WIDTH], cp_ref[...], sp_ref[...]),
        _rope(kvc_ref[0, :, :KV_WIDTH], cq_ref[...], sq_ref[...]),
        _rope(kvn_ref[0, :, :KV_WIDTH], cn_ref[...], sn_ref[...])], axis=0).astype(BF16)
    v = jnp.concatenate([kvp_ref[0, :, KV_WIDTH:], kvc_ref[0, :, KV_WIDTH:], kvn_ref[0, :, KV_WIDTH:]],
                        axis=0).astype(BF16)
    rq = i * BLOCK + lax.broadcasted_iota(jnp.int32, (BLOCK, 1), 0)
    pos_q = (rq + N_META) % padded_len
    c = lax.broadcasted_iota(jnp.int32, (1, 3 * BLOCK), 1)
    blk = (i + nb - 1 + c // BLOCK) % nb
    pos_k = (blk * BLOCK + c % BLOCK + N_META) % padded_len
    mask = (jnp.abs(pos_k - pos_q) <= WINDOW) & (pos_k < seq_len)
    for h in range(N_HEADS):
        kh = h // GROUP
        qh = q[:, h * HEAD_DIM:(h + 1) * HEAD_DIM]
        kk = k[:, kh * HEAD_DIM:(kh + 1) * HEAD_DIM]
        vv = v[:, kh * HEAD_DIM:(kh + 1) * HEAD_DIM]
        s = lax.dot_general(qh, kk, (((1,), (1,)), ((), ())), preferred_element_type=F32)
        s = jnp.where(mask, s, NEG_BIG)
        sink = sink_ref[h]
        m = jnp.maximum(jnp.max(s, axis=-1, keepdims=True), sink)
        p = jnp.exp(s - m)
        denom = jnp.sum(p, axis=-1, keepdims=True) + jnp.exp(sink - m)
        o = jnp.dot(p.astype(BF16), vv, preferred_element_type=F32) / denom
        o_ref[0, :, h * HEAD_DIM:(h + 1) * HEAD_DIM] = o.astype(o_ref.dtype)


def _attention(q, kv, sink, cos_t, sin_t, seq_len):
    B, Lp, _ = q.shape
    nb = Lp // BLOCK
    assert nb >= 3
    kvspec = lambda d: pl.BlockSpec((1, BLOCK, 2 * KV_WIDTH), lambda b, i: (b, (i + nb + d) % nb, 0))
    tspec = lambda d: pl.BlockSpec((BLOCK, LANES), lambda b, i: ((i + nb + d) % nb, 0))
    return pl.pallas_call(
        functools.partial(_attn_kernel, seq_len=seq_len, padded_len=Lp),
        grid=(B, nb),
        in_specs=[pl.BlockSpec(memory_space=pltpu.SMEM),
                  pl.BlockSpec((1, BLOCK, ATTN_WIDTH), lambda b, i: (b, i, 0)),
                  kvspec(-1), kvspec(0), kvspec(1),
                  tspec(0), tspec(0), tspec(-1), tspec(-1), tspec(1), tspec(1)],
        out_specs=pl.BlockSpec((1, BLOCK, ATTN_WIDTH), lambda b, i: (b, i, 0)),
        out_shape=jax.ShapeDtypeStruct((B, Lp, ATTN_WIDTH), BF16),
        compiler_params=_cparams(("parallel", "parallel")),
        name="window_attention",
    )(sink, q, kv, kv, kv, cos_t, sin_t, cos_t, sin_t, cos_t, sin_t)


def _shortconv_kernel(uv_ref, u1_ref, u2_ref, wv_ref, w1_ref, w2_ref, bv_ref, b1_ref, b2_ref,
                      v_ref, vb_ref, x1_ref, x2_ref):
    def conv(u_ref, w_ref, b_ref):
        u = u_ref[0]
        n = u.shape[0]
        prev = pltpu.roll(u, 1, 0)
        nxt = pltpu.roll(u, n - 1, 0)
        return w_ref[0:1, :] * prev + w_ref[1:2, :] * u + w_ref[2:3, :] * nxt + b_ref[...]
    v = conv(uv_ref, wv_ref, bv_ref)
    v_ref[0] = v
    vb_ref[0] = v.astype(BF16)
    x1_ref[0] = conv(u1_ref, w1_ref, b1_ref)
    x2_ref[0] = conv(u2_ref, w2_ref, b2_ref)


def _shortconv(hy, conv_w, conv_b):
    B, Lp, _ = hy.shape
    tc = 128
    nct = HY_WIDTH // tc
    uspec = lambda part: pl.BlockSpec((1, Lp, tc), lambda b, j: (b, 0, part * nct + j))
    wspec = lambda part: pl.BlockSpec((3, tc), lambda b, j: (0, part * nct + j))
    bspec = lambda part: pl.BlockSpec((1, tc), lambda b, j: (0, part * nct + j))
    ospec = pl.BlockSpec((1, Lp, tc), lambda b, j: (b, 0, j))
    f32o = jax.ShapeDtypeStruct((B, Lp, HY_WIDTH), F32)
    return pl.pallas_call(
        _shortconv_kernel,
        grid=(B, nct),
        in_specs=[uspec(0), uspec(1), uspec(2), wspec(0), wspec(1), wspec(2), bspec(0), bspec(1), bspec(2)],
        out_specs=[ospec, ospec, ospec, ospec],
        out_shape=[f32o, jax.ShapeDtypeStruct((B, Lp, HY_WIDTH), BF16), f32o, f32o],
        compiler_params=_cparams(("parallel", "parallel")),
        name="hyena_shortconv",
    )(hy, hy, hy, conv_w, conv_w, conv_w, conv_b, conv_b, conv_b)


def _filter_kernel(z_ref, w1_ref, b1_ref, w2_ref, b2_ref, w3_ref, b3_ref, fr_ref, wo_ref, dl_ref,
                   e_ref, o_ref, *, seq_len, padded_len):
    hp = lax.Precision.HIGHEST
    i = pl.program_id(0)
    tr = z_ref.shape[0]
    z = z_ref[...]
    fr = fr_ref[...]
    a = jnp.sin(fr * (jnp.dot(z, w1_ref[...], precision=hp, preferred_element_type=F32) + b1_ref[...]))
    a = jnp.sin(fr * (jnp.dot(a, w2_ref[...], precision=hp, preferred_element_type=F32) + b2_ref[...]))
    a = jnp.sin(fr * (jnp.dot(a, w3_ref[...], precision=hp, preferred_element_type=F32) + b3_ref[...]))
    h = jnp.dot(a, wo_ref[...], precision=hp, preferred_element_type=F32)
    t = z[:, 0:1]
    decay = jnp.exp(-t * dl_ref[...])
    decay = jnp.concatenate([decay] * HY_ORDER, axis=-1)
    ow = HY_ORDER * HY_WIDTH
    hf = h[:, :ow] * decay
    hb = h[:, ow:] * decay
    lag = (i * tr + lax.broadcasted_iota(jnp.int32, (tr, 1), 0) + N_META) % padded_len
    valid = lag < seq_len
    zero_lag = lag == 0
    e_ref[...] = jnp.where(valid, hf + hb, 0.0)
    o_ref[...] = jnp.where(valid, jnp.where(zero_lag, hf + hb, hf - hb), 0.0)


def _filters(L, Lp, w1, b1, w2, b2, w3, b3, freq, wout):
    r = jnp.arange(Lp, dtype=jnp.int32)
    lag = ((r + N_META) % Lp).astype(F32)
    t = lag / max(L - 1, 1)
    f = jnp.linspace(1e-4, FILT_BANDS - 1, FILT_BANDS, dtype=F32)
    ang = (2.0 * math.pi / L) * lag[:, None] * f[None]
    z = jnp.concatenate([t[:, None], jnp.cos(ang), -jnp.sin(ang)], axis=-1)
    z = jnp.pad(z, ((0, 0), (0, LANES - FILT_EMB)))
    w1p = jnp.pad(w1, ((0, LANES - FILT_EMB), (0, 0)))
    max_decay = math.log(DECAY_TARGET) / FAST_DECAY_PCT
    min_decay = math.log(DECAY_TARGET) / SLOW_DECAY_PCT
    deltas = jnp.abs(jnp.linspace(min_decay, max_decay, HY_WIDTH, dtype=F32))[None]
    tr = _pick_tile(Lp, 512, 8)
    full = lambda a: pl.BlockSpec(a.shape, lambda i: (0,) * a.ndim)
    args = [w1p, b1[None], w2, b2[None], w3, b3[None], freq[None], wout, deltas]
    ow = HY_ORDER * HY_WIDTH
    return pl.pallas_call(
        functools.partial(_filter_kernel, seq_len=L, padded_len=Lp),
        grid=(Lp // tr,),
        in_specs=[pl.BlockSpec((tr, LANES), lambda i: (i, 0))] + [full(a) for a in args],
        out_specs=[pl.BlockSpec((tr, ow), lambda i: (i, 0))] * 2,
        out_shape=[jax.ShapeDtypeStruct((Lp, ow), F32)] * 2,
        compiler_params=_cparams(("parallel",)),
        name="hyena_filter_mlp",
    )(z, *args)


def _dft_mats(L, Lp):
    mh = Lp
    n = 2 * mh
    r = jnp.arange(Lp, dtype=jnp.int32)
    t = (r + N_META) % Lp
    valid = (t < L)[None, :]
    f = jnp.arange(mh, dtype=jnp.int32)[:, None]
    fb = 64
    assert mh % fb == 0
    angle = lambda k: ((k[:, None] * t[None, :]) % n).astype(F32) * (2.0 * math.pi / n)
    ang_h = angle(jnp.arange(mh // fb, dtype=jnp.int32) * fb)[:, None, :]
    ang_l = angle(jnp.arange(fb, dtype=jnp.int32))[None, :, :]
    ch, sh, cl, sl = jnp.cos(ang_h), jnp.sin(ang_h), jnp.cos(ang_l), jnp.sin(ang_l)
    c = (ch * cl - sh * sl).reshape(mh, Lp)
    s = (sh * cl + ch * sl).reshape(mh, Lp)
    nyq = jnp.where(t % 2 == 0, 1.0, -1.0).astype(F32)[None, :]
    fc = jnp.where(valid, c, 0.0)
    fs = jnp.where(valid, jnp.where(f == 0, nyq, -s), 0.0)
    wgt = jnp.where(f == 0, 1.0 / n, 2.0 / n).astype(F32)
    gc = (fc * wgt).T
    gs = (fs * wgt).T
    return fc.astype(BF16), fs.astype(BF16), gc.astype(BF16), gs.astype(BF16)


def _dft_plain_kernel(fc_ref, fs_ref, z_ref, re_ref, im_ref):
    z = z_ref[0]
    re_ref[0] = jnp.dot(fc_ref[...], z, preferred_element_type=F32)
    im_ref[0] = jnp.dot(fs_ref[...], z, preferred_element_type=F32)


def _dft_plain(fc, fs, z):
    nbatch, Lp, width = z.shape
    mh = fc.shape[0]
    tm = _pick_tile(mh, 512, 16)
    tn = 512
    ospec = pl.BlockSpec((1, tm, tn), lambda i, b, j: (b, i, j))
    osh = jax.ShapeDtypeStruct((nbatch, mh, width), F32)
    return pl.pallas_call(
        _dft_plain_kernel,
        grid=(mh // tm, nbatch, width // tn),
        in_specs=[pl.BlockSpec((tm, Lp), lambda i, b, j: (i, 0)), pl.BlockSpec((tm, Lp), lambda i, b, j: (i, 0)),
                  pl.BlockSpec((1, Lp, tn), lambda i, b, j: (b, 0, j))],
        out_specs=[ospec, ospec],
        out_shape=[osh, osh],
        compiler_params=_cparams(("parallel", "parallel", "parallel")),
        name="hyena_filter_dft",
    )(fc, fs, z)


def _dft_mul_kernel(fc_ref, fs_ref, z_ref, kr_ref, ki_ref, re_ref, im_ref):
    i = pl.program_id(0)
    z = z_ref[0]
    re = jnp.dot(fc_ref[...], z, preferred_element_type=F32)
    im = jnp.dot(fs_ref[...], z, preferred_element_type=F32)
    kr = kr_ref[...]
    ki = ki_ref[...]
    row0 = (i * re.shape[0] + lax.broadcasted_iota(jnp.int32, (re.shape[0], 1), 0)) == 0
    ki_x = jnp.where(row0, 0.0, ki)
    kd = jnp.where(row0, ki, kr)
    re_ref[0] = (re * kr - im * ki_x).astype(BF16)
    im_ref[0] = (re * ki_x + im * kd).astype(BF16)


def _dft_mul(fc, fs, zb, kr, ki, order):
    B, Lp, width = zb.shape
    mh = fc.shape[0]
    tm = _pick_tile(mh, 512, 16)
    ospec = pl.BlockSpec((1, tm, width), lambda i, b: (b, i, 0))
    osh = jax.ShapeDtypeStruct((B, mh, width), BF16)
    return pl.pallas_call(
        _dft_mul_kernel,
        grid=(mh // tm, B),
        in_specs=[pl.BlockSpec((tm, Lp), lambda i, b: (i, 0)), pl.BlockSpec((tm, Lp), lambda i, b: (i, 0)),
                  pl.BlockSpec((1, Lp, width), lambda i, b: (b, 0, 0)),
                  pl.BlockSpec((tm, width), lambda i, b: (i, order)),
                  pl.BlockSpec((tm, width), lambda i, b: (i, order))],
        out_specs=[ospec, ospec],
        out_shape=[osh, osh],
        compiler_params=_cparams(("parallel", "parallel")),
        name="hyena_dft_mul",
    )(fc, fs, zb, kr, ki)


def _idft_gate_kernel(gc_ref, gs_ref, re_ref, im_ref, gate_ref, z_ref, sk_ref, zo_ref, zb_ref):
    conv = jnp.dot(gc_ref[...], re_ref[0], preferred_element_type=F32)
    conv += jnp.dot(gs_ref[...], im_ref[0], preferred_element_type=F32)
    zn = gate_ref[0] * (conv + sk_ref[...] * z_ref[0])
    zo_ref[0] = zn
    zb_ref[0] = zn.astype(BF16)


def _idft_gate(gc, gs, sre, sim, gate, z, skip, order):
    B, mh, width = sre.shape
    Lp = gc.shape[0]
    tm = _pick_tile(Lp, 512, 16)
    rspec = pl.BlockSpec((1, tm, width), lambda i, b: (b, i, 0))
    sspec = pl.BlockSpec((1, mh, width), lambda i, b: (b, 0, 0))
    gspec = pl.BlockSpec((tm, mh), lambda i, b: (i, 0))
    return pl.pallas_call(
        _idft_gate_kernel,
        grid=(Lp // tm, B),
        in_specs=[gspec, gspec, sspec, sspec, rspec, rspec, pl.BlockSpec((1, width), lambda i, b: (0, 0))],
        out_specs=[rspec, rspec],
        out_shape=[jax.ShapeDtypeStruct((B, Lp, width), F32), jax.ShapeDtypeStruct((B, Lp, width), BF16)],
        compiler_params=_cparams(("parallel", "parallel")),
        name="hyena_idft_gate",
    )(gc, gs, sre, sim, gate, z, skip[order][None])


def _hyena(hy, L, conv_w, conv_b, fw1, fb1, fw2, fb2, fw3, fb3, freq, wout, skip):
    B, Lp, _ = hy.shape
    v, vb, x1, x2 = _shortconv(hy, conv_w, conv_b[None])
    fc, fs, gc, gs = _dft_mats(L, Lp)
    e, o = _filters(L, Lp, fw1, fb1, fw2, fb2, fw3, fb3, freq, wout)
    eo = jnp.stack([e, o])
    hi = eo.astype(BF16)
    lo = (eo - hi.astype(F32)).astype(BF16)
    sre, sim = _dft_plain(fc, fs, jnp.concatenate([hi, lo], axis=0))
    sre = sre[:2] + sre[2:]
    sim = sim[:2] + sim[2:]
    row0 = (jnp.arange(Lp) == 0)[:, None]
    kr = sre[0]
    ki = jnp.where(row0, sim[0], sim[1])
    z, zb = v, vb
    for order, gate in enumerate((x1, x2)):
        pre, pim = _dft_mul(fc, fs, zb, kr, ki, order)
        z, zb = _idft_gate(gc, gs, pre, pim, gate, z, skip, order)
    return zb


EXT_WIDTH = D_MODEL + LANES


def _merge_kernel(a_ref, ho_ref, gate_ref, h_ref, wa_ref, wh_ref, wo_ref, gf_ref, wr_ref, *rest):
    hnew_ref, ext_ref, afft_ref = rest[-3:]
    ga = jax.nn.sigmoid(gate_ref[:, :D_MODEL])
    gh = jax.nn.sigmoid(gate_ref[:, D_MODEL:])
    merged = ga * jnp.dot(a_ref[...], wa_ref[...], preferred_element_type=F32)
    merged += gh * jnp.dot(ho_ref[...], wh_ref[...], preferred_element_type=F32)
    h = h_ref[...] + jnp.dot(merged.astype(BF16), wo_ref[...], preferred_element_type=F32)
    hnew_ref[...] = h
    hn = _rms(h, gf_ref[...])
    logits = jnp.dot(hn, wr_ref[...], precision=lax.Precision.HIGHEST, preferred_element_type=F32)
    lane = lax.broadcasted_iota(jnp.int32, logits.shape, 1)
    logits = jnp.where(lane < N_EXPERTS, logits, NEG_BIG)
    m = jnp.max(logits, axis=-1, keepdims=True)
    p = jnp.exp(logits - m)
    aff = p / jnp.sum(p, axis=-1, keepdims=True)
    ext_ref[:, :D_MODEL] = hn
    ext_ref[:, D_MODEL:] = aff
    afft_ref[...] = aff.T[:N_EXPERTS]


def _merge(a2, ho2, gates, h2, wa, wh, wo, gf, wr, ext_prev, row_base, rows_total):
    rows = h2.shape[0]
    tm = _pick_tile(rows, 256, 128)
    assert row_base % tm == 0
    row = lambda w_: pl.BlockSpec((tm, w_), lambda i: (i, 0))
    full = lambda a: pl.BlockSpec(a.shape, lambda i: (0,) * a.ndim)
    args = [a2, ho2, gates, h2, wa, wh, wo, gf, wr]
    in_specs = [row(ATTN_WIDTH), row(HY_WIDTH), row(2 * D_MODEL), row(D_MODEL),
                full(wa), full(wh), full(wo), full(gf), full(wr)]
    aliases = {}
    if ext_prev is not None:
        args.append(ext_prev)
        in_specs.append(pl.BlockSpec(memory_space=pl.ANY))
        aliases = {len(args) - 1: 1}
    return pl.pallas_call(
        _merge_kernel,
        grid=(rows // tm,),
        in_specs=in_specs,
        out_specs=[row(D_MODEL), pl.BlockSpec((tm, EXT_WIDTH), lambda i: (i + row_base // tm, 0)),
                   pl.BlockSpec((N_EXPERTS, tm), lambda i: (0, i))],
        out_shape=[jax.ShapeDtypeStruct((rows, D_MODEL), F32), jax.ShapeDtypeStruct((rows_total, EXT_WIDTH), F32),
                   jax.ShapeDtypeStruct((N_EXPERTS, rows), F32)],
        input_output_aliases=aliases,
        compiler_params=_cparams(("arbitrary",)),
        name="merge_router",
    )(*args)


ROUTE_CHUNK = 256
SLOT_TILE = 128


def _select_kernel(aff_ref, triu_ref, tril_ref, incl_ref, dval_ref, start_ref, end_ref, bits_ref,
                   *, cap, seq_tokens, padded_len, y_base):
    rows = aff_ref.shape[1]
    ch = triu_ref.shape[0]
    lane = lax.broadcasted_iota(jnp.int32, (1, rows), 1)
    rl = lane % padded_len
    valid = (rl < seq_tokens) | (rl >= padded_len - N_META)
    bits_ref[...] = jnp.where(valid, aff_ref[...], -1.0)

    def count_ge(cand_bits):
        cand = pltpu.bitcast(cand_bits, F32)
        return jnp.sum(jnp.where(bits_ref[...] >= cand, 1.0, 0.0), axis=1, keepdims=True)

    def bisect(k, tb):
        cand = tb | jnp.left_shift(jnp.int32(1), 30 - k)
        return jnp.where(count_ge(cand) >= cap, cand, tb)

    thr = pltpu.bitcast(lax.fori_loop(0, 31, bisect, jnp.zeros((N_EXPERTS, 1), jnp.int32)), F32)
    need = cap - jnp.sum(jnp.where(bits_ref[...] > thr, 1.0, 0.0), axis=1, keepdims=True)
    triu = triu_ref[...]
    tril = tril_ref[...]

    def chunk(c, carry):
        ceq, csel, crow = carry
        sl = pl.ds(pl.multiple_of(c * ch, ch), ch)
        b = bits_ref[:, sl]
        eq = b == thr
        incl_eq = jnp.dot(jnp.where(eq, 1.0, 0.0).astype(BF16), triu, preferred_element_type=F32) + ceq
        sel = (b > thr) | (eq & (incl_eq <= need))
        self_ = jnp.where(sel, 1.0, 0.0)
        selb = self_.astype(BF16)
        incl = jnp.dot(selb, triu, preferred_element_type=F32) + csel
        kt = jnp.sum(self_, axis=0, keepdims=True)
        kt8 = jnp.broadcast_to(kt, (8, ch)).astype(BF16)
        excl = jnp.dot(kt8, triu, preferred_element_type=F32)[0:1] - kt + crow
        rank = jnp.dot(tril, selb, preferred_element_type=F32)
        incl_ref[:, sl] = jnp.where(sel, incl, -incl - 1.0)
        dval_ref[:, sl] = excl + rank
        start_ref[:, sl] = excl
        end_ref[:, sl] = excl + kt
        return (ceq + jnp.sum(jnp.where(eq, 1.0, 0.0), axis=1, keepdims=True),
                csel + jnp.sum(self_, axis=1, keepdims=True),
                crow + jnp.sum(kt, axis=1, keepdims=True))

    zero = jnp.zeros((N_EXPERTS, 1), F32)
    lax.fori_loop(0, rows // ch, chunk, (zero, zero, jnp.full((1, 1), y_base, F32)))


def _select(afft, cap, seq_tokens, padded_len, y_base):
    rows = afft.shape[1]
    ch = ROUTE_CHUNK
    assert rows % ch == 0
    triu = (jnp.arange(ch)[:, None] <= jnp.arange(ch)[None, :]).astype(BF16)
    tril = (jnp.arange(N_EXPERTS)[:, None] > jnp.arange(N_EXPERTS)[None, :]).astype(BF16)
    full = lambda shape: pl.BlockSpec(shape, lambda i: (0,) * len(shape))
    return pl.pallas_call(
        functools.partial(_select_kernel, cap=cap, seq_tokens=seq_tokens, padded_len=padded_len, y_base=y_base),
        grid=(1,),
        in_specs=[full((N_EXPERTS, rows)), full((ch, ch)), full((N_EXPERTS, N_EXPERTS))],
        out_specs=[full((N_EXPERTS, rows)), full((N_EXPERTS, rows)), full((1, rows)), full((1, rows))],
        out_shape=[jax.ShapeDtypeStruct((N_EXPERTS, rows), F32), jax.ShapeDtypeStruct((N_EXPERTS, rows), F32),
                   jax.ShapeDtypeStruct((1, rows), F32), jax.ShapeDtypeStruct((1, rows), F32)],
        scratch_shapes=[pltpu.VMEM((N_EXPERTS, rows), F32)],
        compiler_params=_cparams(("arbitrary",)),
        name="expert_select",
    )(afft, triu, tril)


def _lists_kernel(offs_ref, incl_ref, dval_ref, idx_ref, dst_ref, *, cap, nchunks, row_base, dump_base):
    ch = ROUTE_CHUNK
    t = SLOT_TILE
    cp = idx_ref.shape[0]
    slot_all = lax.broadcasted_iota(jnp.int32, (cp, LANES), 0)
    lane_all = lax.broadcasted_iota(jnp.int32, (cp, LANES), 1)
    idx_ref[...] = jnp.full((cp, LANES), row_base, jnp.int32)
    dst_ref[...] = dump_base + jnp.maximum(slot_all - cap, 0) * N_EXPERTS + jnp.minimum(lane_all, N_EXPERTS - 1)
    lane = lax.broadcasted_iota(jnp.int32, (t, LANES), 1)
    tok_lane = lax.broadcasted_iota(jnp.int32, (1, ch), 1)

    def chunk(c, _):
        csl = pl.ds(pl.multiple_of(c * ch, ch), ch)
        tok = (row_base + c * ch + tok_lane).astype(F32)
        for e in range(N_EXPERTS):
            o0 = offs_ref[e * (nchunks + 1) + c]
            o1 = offs_ref[e * (nchunks + 1) + c + 1]
            row_incl = incl_ref[e:e + 1, csl]
            row_dval = dval_ref[e:e + 1, csl]

            def tile(j, _):
                sl = pl.ds(pl.multiple_of(j * t, t), t)
                slot = j * t + lax.broadcasted_iota(jnp.int32, (t, 1), 0)
                hit = row_incl == (slot + 1).astype(F32)
                src = jnp.sum(jnp.where(hit, tok, 0.0), axis=1, keepdims=True).astype(jnp.int32)
                dst = jnp.sum(jnp.where(hit, row_dval, 0.0), axis=1, keepdims=True).astype(jnp.int32)
                m = (slot >= o0) & (slot < o1) & (lane == e)
                idx_ref[sl, :] = jnp.where(m, src, idx_ref[sl, :])
                dst_ref[sl, :] = jnp.where(m, dst, dst_ref[sl, :])
                return 0

            lax.fori_loop(o0 // t, (o1 + t - 1) // t, tile, 0)
        return 0

    lax.fori_loop(0, nchunks, chunk, 0)


def _lists(offs, incl, dval, cap, cp, row_base, dump_base):
    rows = incl.shape[1]
    nchunks = rows // ROUTE_CHUNK
    full = lambda shape: pl.BlockSpec(shape, lambda i, o: (0,) * len(shape))
    return pl.pallas_call(
        functools.partial(_lists_kernel, cap=cap, nchunks=nchunks, row_base=row_base, dump_base=dump_base),
        grid_spec=pltpu.PrefetchScalarGridSpec(
            num_scalar_prefetch=1, grid=(1,),
            in_specs=[full((N_EXPERTS, rows)), full((N_EXPERTS, rows))],
            out_specs=[full((cp, LANES)), full((cp, LANES))]),
        out_shape=[jax.ShapeDtypeStruct((cp, LANES), jnp.int32)] * 2,
        compiler_params=_cparams(("arbitrary",)),
        name="expert_slot_lists",
    )(offs, incl, dval)


FFN_TILE = 256


def _ffn_kernel(idx_hbm, dst_hbm, ext_hbm, wg_ref, wu_ref, wd_ref, y_hbm,
                xbuf, ybuf, idx_sm, dst_sm, gsem, ssem, isem, dsem):
    e = pl.program_id(0)
    i = pl.program_id(1)
    nt = pl.num_programs(1)
    total = pl.num_programs(0) * nt
    s = e * nt + i
    slot = s % 2
    tm = FFN_TILE

    def idx_copy(step, buf):
        return pltpu.make_async_copy(idx_hbm.at[step], idx_sm.at[buf], isem)

    def dst_copy(step):
        return pltpu.make_async_copy(dst_hbm.at[step], dst_sm, dsem)

    def gather_row(buf, r):
        pltpu.make_async_copy(ext_hbm.at[pl.ds(idx_sm[buf, r], 1)], xbuf.at[buf, pl.ds(r, 1)], gsem.at[buf]).start()

    def scatter_row(r):
        pltpu.make_async_copy(ybuf.at[pl.ds(r, 1)], y_hbm.at[pl.ds(dst_sm[r], 1)], ssem).start()

    def gather_wait(buf):
        pltpu.make_async_copy(ext_hbm.at[pl.ds(0, tm)], xbuf.at[buf], gsem.at[buf]).wait()

    def scatter_wait():
        pltpu.make_async_copy(ybuf, y_hbm.at[pl.ds(0, tm)], ssem).wait()

    @pl.when(s == 0)
    def _():
        c = idx_copy(0, 0)
        c.start()
        c.wait()

        def body(r, _):
            gather_row(0, r)
            return 0
        lax.fori_loop(0, tm, body, 0, unroll=8)
        ybuf[...] = jnp.zeros_like(ybuf)

    nxt = jnp.minimum(s + 1, total - 1)
    ic = idx_copy(nxt, 1 - slot)
    dc = dst_copy(s)
    ic.start()
    dc.start()
    gather_wait(slot)
    ic.wait()
    dc.wait()

    parts = 4
    per = tm // parts

    def issue(part):
        for r in range(part * per, (part + 1) * per):
            gather_row(1 - slot, r)
            scatter_row(r)

    x = xbuf[slot]
    lane = lax.broadcasted_iota(jnp.int32, (tm, LANES), 1)
    gate = jnp.sum(jnp.where(lane == e, x[:, D_MODEL:], 0.0), axis=1, keepdims=True)
    xb = x[:, :D_MODEL].astype(BF16)
    issue(0)
    gt = jnp.dot(xb, wg_ref[0], preferred_element_type=F32)
    issue(1)
    up = jnp.dot(xb, wu_ref[0], preferred_element_type=F32)
    issue(2)
    hid = (gt * jax.nn.sigmoid(gt) * up).astype(BF16)
    y = jnp.dot(hid, wd_ref[0], preferred_element_type=F32) * gate
    issue(3)
    scatter_wait()
    ybuf[...] = y

    @pl.when(s == total - 1)
    def _():
        gather_wait(1 - slot)
        c = dst_copy(total)
        c.start()
        c.wait()

        def body(r, _):
            scatter_row(r)
            return 0
        lax.fori_loop(0, tm, body, 0, unroll=8)
        scatter_wait()


def _ffn(idx, dst, ext, wg, wu, wd, y_rows):
    E = wg.shape[0]
    nt = idx.shape[0] // E
    tm = FFN_TILE
    anyspec = pl.BlockSpec(memory_space=pl.ANY)
    return pl.pallas_call(
        _ffn_kernel,
        grid=(E, nt),
        in_specs=[anyspec, anyspec, anyspec,
                  pl.BlockSpec((1, D_MODEL, D_EXPERT), lambda e, i: (e, 0, 0)),
                  pl.BlockSpec((1, D_MODEL, D_EXPERT), lambda e, i: (e, 0, 0)),
                  pl.BlockSpec((1, D_EXPERT, D_MODEL), lambda e, i: (e, 0, 0))],
        out_specs=anyspec,
        out_shape=jax.ShapeDtypeStruct((y_rows, D_MODEL), F32),
        scratch_shapes=[pltpu.VMEM((2, tm, EXT_WIDTH), F32), pltpu.VMEM((tm, D_MODEL), F32),
                        pltpu.SMEM((2, tm), jnp.int32), pltpu.SMEM((tm,), jnp.int32),
                        pltpu.SemaphoreType.DMA((2,)), pltpu.SemaphoreType.DMA,
                        pltpu.SemaphoreType.DMA, pltpu.SemaphoreType.DMA],
        compiler_params=_cparams(("arbitrary", "arbitrary")),
        name="expert_ffn",
    )(idx, dst, ext, wg, wu, wd)


COMBINE_CHUNK = 128
Y_BLOCK = 256


def _combine_kernel(cg_ref, bi_ref, ci_ref, blk_ref, flag_ref, y_ref, start_ref, end_ref, h_ref, g_ref, o_ref, acc_ref,
                    *, y_limit):
    w = pl.program_id(0)
    flags = flag_ref[w]

    @pl.when((flags & 1) != 0)
    def _():
        acc_ref[...] = jnp.zeros_like(acc_ref)

    @pl.when((flags & 4) != 0)
    def _():
        tc = COMBINE_CHUNK
        st = jnp.broadcast_to(start_ref[...], (tc, tc)).T
        en = jnp.broadcast_to(end_ref[...], (tc, tc)).T
        reps = Y_BLOCK // tc
        st = jnp.concatenate([st] * reps, axis=1)
        en = jnp.concatenate([en] * reps, axis=1)
        jg = (blk_ref[w] * Y_BLOCK + lax.broadcasted_iota(jnp.int32, (1, Y_BLOCK), 1)).astype(F32)
        onehot = jnp.where((jg >= st) & (jg < en), 1.0, 0.0).astype(BF16)
        jrow = blk_ref[w] * Y_BLOCK + lax.broadcasted_iota(jnp.int32, (Y_BLOCK, 1), 0)
        y = jnp.where(jrow < y_limit, y_ref[...], 0.0)
        hi = y.astype(BF16)
        lo = (y - hi.astype(F32)).astype(BF16)
        acc_ref[...] += (jnp.dot(onehot, hi, preferred_element_type=F32)
                         + jnp.dot(onehot, lo, preferred_element_type=F32))

    @pl.when((flags & 2) != 0)
    def _():
        o_ref[0] = _rms(h_ref[...] + acc_ref[...], g_ref[...])


def _combine(work, y, start, end, hnew, g, B, S, y_limit):
    cg, bi, ci, blk, flags = work
    nw = cg.shape[0]
    tc = COMBINE_CHUNK
    assert S % tc == 0 and hnew.shape[0] % tc == 0
    return pl.pallas_call(
        functools.partial(_combine_kernel, y_limit=y_limit),
        grid_spec=pltpu.PrefetchScalarGridSpec(
            num_scalar_prefetch=5, grid=(nw,),
            in_specs=[pl.BlockSpec((Y_BLOCK, D_MODEL), lambda w, cg, bi, ci, blk, fl: (blk[w], 0)),
                      pl.BlockSpec((1, tc), lambda w, cg, bi, ci, blk, fl: (0, cg[w])),
                      pl.BlockSpec((1, tc), lambda w, cg, bi, ci, blk, fl: (0, cg[w])),
                      pl.BlockSpec((tc, D_MODEL), lambda w, cg, bi, ci, blk, fl: (cg[w], 0)),
                      pl.BlockSpec((1, D_MODEL), lambda w, cg, bi, ci, blk, fl: (0, 0))],
            out_specs=pl.BlockSpec((1, tc, D_MODEL), lambda w, cg, bi, ci, blk, fl: (bi[w], ci[w], 0)),
            scratch_shapes=[pltpu.VMEM((tc, D_MODEL), F32)]),
        out_shape=jax.ShapeDtypeStruct((B, S, D_MODEL), F32),
        compiler_params=_cparams(("arbitrary",)),
        name="combine_final_norm",
    )(cg, bi, ci, blk, flags, y, start, end, hnew, g)


def _combine_worklist(start, end, B, S, Lp, n_contrib):
    tc = COMBINE_CHUNK
    cpb = S // tc
    b = jnp.repeat(jnp.arange(B, dtype=jnp.int32), cpb)
    ci = jnp.tile(jnp.arange(cpb, dtype=jnp.int32), B)
    cg = b * (Lp // tc) + ci
    lo = start[0, cg * tc].astype(jnp.int32)
    hi = end[0, cg * tc + tc - 1].astype(jnp.int32)
    nblk = jnp.where(hi > lo, (hi - 1) // Y_BLOCK - lo // Y_BLOCK + 1, 1)
    cum = jnp.cumsum(nblk)
    nchunks = B * cpb
    nw = nchunks + n_contrib // Y_BLOCK + 2
    w = jnp.arange(nw, dtype=jnp.int32)
    valid = w < cum[-1]
    c = jnp.minimum(jnp.sum((cum[None, :] <= w[:, None]).astype(jnp.int32), axis=1), nchunks - 1)
    k = w - (cum[c] - nblk[c])
    k = jnp.where(valid, k, nblk[c] - 1)
    blk = lo[c] // Y_BLOCK + k
    flags = jnp.where(valid, (k == 0) * 1 + (k == nblk[c] - 1) * 2 + 4, 0).astype(jnp.int32)
    return cg[c], b[c], ci[c], blk.astype(jnp.int32), flags


def _rope_tables(Lp):
    half = HEAD_DIM // 2
    inv = 1.0 / (ROPE_THETA ** (jnp.arange(half, dtype=F32) / half))
    pos = ((jnp.arange(Lp, dtype=jnp.int32) + N_META) % Lp).astype(F32)
    ang = pos[:, None] * inv[None]
    cos = jnp.cos(ang)
    sin = jnp.sin(ang)
    reps = LANES // HEAD_DIM
    cos_t = jnp.concatenate([cos, cos] * reps, axis=-1)
    sin_t = jnp.concatenate([-sin, sin] * reps, axis=-1)
    return cos_t, sin_t


def _dims(x):
    B, S, _ = x.shape
    L = S + N_META
    Lp = (L // BLOCK + 1) * BLOCK
    return B, S, L, Lp


def _mixer(x, p, ext_prev, row_base, rows_total):
    B, S, L, Lp = _dims(x)
    meta = jnp.broadcast_to(p["meta_tokens"][None], (B, N_META, D_MODEL))
    h = jnp.concatenate([x, jnp.zeros((B, Lp - L, D_MODEL), F32), meta], axis=1)
    h2 = h.reshape(B * Lp, D_MODEL)
    q, kv, hy, gates = _inproj(h2, p["norm_mix_g"], p["w_in"])
    cos_t, sin_t = _rope_tables(Lp)
    a = _attention(q.reshape(B, Lp, -1), kv.reshape(B, Lp, -1), p["attn_sink"], cos_t, sin_t, L)
    ho = _hyena(hy.reshape(B, Lp, -1), L, p["hy_conv_w"], p["hy_conv_b"], p["hy_filt_w1"], p["hy_filt_b1"],
                p["hy_filt_w2"], p["hy_filt_b2"], p["hy_filt_w3"], p["hy_filt_b3"], p["hy_filt_freq"],
                p["hy_filt_wout"], p["hy_skip"])
    return _merge(a.reshape(B * Lp, -1), ho.reshape(B * Lp, -1), gates, h2,
                  p["w_attn_up"], p["w_hyena_up"], p["w_out"], p["norm_ffn_g"], p["w_router"],
                  ext_prev, row_base, rows_total)


def _route(afft, dims, row_base, y_base, dump_base):
    B, S, L, Lp = dims
    cap = max(1, (CAPACITY_FACTOR * B * L) // N_EXPERTS)
    cp = -(-cap // FFN_TILE) * FFN_TILE
    incl, dval, start, end = _select(afft, cap, S, Lp, y_base)
    last = incl[:, ROUTE_CHUNK - 1::ROUTE_CHUNK]
    counts = jnp.where(last >= 0, last, -last - 1.0).astype(jnp.int32)
    offs = jnp.concatenate([jnp.zeros((N_EXPERTS, 1), jnp.int32), counts], axis=1).reshape(-1)
    idx_t, dst_t = _lists(offs, incl, dval, cap, cp, row_base, dump_base)
    return idx_t[:, :N_EXPERTS].T, dst_t[:, :N_EXPERTS].T, start, end, cap, cp


def kernel(x_prompt, x_sample, meta_tokens, norm_mix_g, w_in, attn_sink, hy_conv_w, hy_conv_b, hy_filt_w1, hy_filt_b1, hy_filt_w2, hy_filt_b2, hy_filt_w3, hy_filt_b3, hy_filt_freq, hy_filt_wout, hy_skip, w_attn_up, w_hyena_up, w_out, norm_ffn_g, w_router, w_exp_gate, w_exp_up, w_exp_down, norm_final_g):
    p = dict(
        meta_tokens=meta_tokens, norm_mix_g=norm_mix_g[0][None], w_in=w_in[0].astype(BF16), attn_sink=attn_sink[0],
        hy_conv_w=hy_conv_w[0], hy_conv_b=hy_conv_b[0], hy_filt_w1=hy_filt_w1[0], hy_filt_b1=hy_filt_b1[0],
        hy_filt_w2=hy_filt_w2[0], hy_filt_b2=hy_filt_b2[0], hy_filt_w3=hy_filt_w3[0], hy_filt_b3=hy_filt_b3[0],
        hy_filt_freq=hy_filt_freq[0], hy_filt_wout=hy_filt_wout[0], hy_skip=hy_skip[0],
        w_attn_up=w_attn_up[0].astype(BF16), w_hyena_up=w_hyena_up[0].astype(BF16), w_out=w_out[0].astype(BF16),
        norm_ffn_g=norm_ffn_g[0][None],
        w_router=jnp.pad(w_router[0], ((0, 0), (0, ROUTER_PAD - N_EXPERTS))),
    )
    xs = (x_prompt, x_sample)
    dims = [_dims(x) for x in xs]
    rows = [B * Lp for (B, S, L, Lp) in dims]
    rows_total = sum(rows)
    caps = [max(1, (CAPACITY_FACTOR * B * L) // N_EXPERTS) for (B, S, L, Lp) in dims]
    cps = [-(-c // FFN_TILE) * FFN_TILE for c in caps]
    y_bases = [0, N_EXPERTS * caps[0]]
    n_contrib = N_EXPERTS * sum(caps)
    dump_bases = [n_contrib, n_contrib + N_EXPERTS * (cps[0] - caps[0])]
    y_rows = n_contrib + N_EXPERTS * sum(cp - c for cp, c in zip(cps, caps))
    spare_base = y_rows
    y_rows = -(-(y_rows + FFN_TILE) // Y_BLOCK) * Y_BLOCK

    ext = jnp.zeros((rows_total, EXT_WIDTH), F32)
    hnews, routes = [], []
    row_base = 0
    for g, x in enumerate(xs):
        hnew, ext, afft = _mixer(x, p, ext, row_base, rows_total)
        hnews.append(hnew)
        routes.append(_route(afft, dims[g], row_base, y_bases[g], dump_bases[g]))
        row_base += rows[g]
    idx = jnp.concatenate([r[0] for r in routes], axis=1).reshape(-1, FFN_TILE)
    dst = jnp.concatenate([r[1] for r in routes], axis=1).reshape(-1, FFN_TILE)
    dst = jnp.concatenate([spare_base + jnp.arange(FFN_TILE, dtype=jnp.int32)[None], dst], axis=0)
    dst = jnp.pad(dst, ((0, -dst.shape[0] % 8), (0, 0)))
    y = _ffn(idx, dst, ext, w_exp_gate[0].astype(BF16), w_exp_up[0].astype(BF16), w_exp_down[0].astype(BF16), y_rows)
    outs = []
    for g in range(len(xs)):
        B, S, L, Lp = dims[g]
        _, _, start, end, cap, _ = routes[g]
        work = _combine_worklist(start, end, B, S, Lp, N_EXPERTS * cap)
        outs.append(_combine(work, y, start, end, hnews[g], norm_final_g[None], B, S, n_contrib))
    return tuple(outs)
```

```python
import functools
import math

import jax
import jax.numpy as jnp
from jax import lax
from jax.experimental import pallas as pl
from jax.experimental.pallas import tpu as pltpu

F32 = jnp.float32
BF16 = jnp.bfloat16

D_MODEL = 1024
N_META = 16
HEAD_DIM = 64
N_HEADS = 8
N_KV = 2
GROUP = N_HEADS // N_KV
ATTN_WIDTH = N_HEADS * HEAD_DIM
KV_WIDTH = N_KV * HEAD_DIM
WINDOW = 128
BLOCK = 128
ROPE_THETA = 10000.0
HY_WIDTH = D_MODEL // 2
HY_ORDER = 2
HY_PROJ = (HY_ORDER + 1) * HY_WIDTH
FILT_EMB = 33
FILT_BANDS = (FILT_EMB - 1) // 2
FILT_HIDDEN = 64
DECAY_TARGET = 1e-2
FAST_DECAY_PCT = 0.3
SLOW_DECAY_PCT = 1.5
N_EXPERTS = 16
CAPACITY_FACTOR = 2
D_EXPERT = 2048
EPS = 1e-6
Q0, K0, V0, HY0, GA0, GH0 = 0, ATTN_WIDTH, ATTN_WIDTH + KV_WIDTH, ATTN_WIDTH + 2 * KV_WIDTH, \
    ATTN_WIDTH + 2 * KV_WIDTH + HY_PROJ, ATTN_WIDTH + 2 * KV_WIDTH + HY_PROJ + D_MODEL
IN_COLS = GH0 + D_MODEL

LANES = 128
ROUTER_PAD = LANES
VMEM_LIMIT = 56 * 1024 * 1024
NEG_BIG = -1e30


def _cparams(sem):
    return pltpu.CompilerParams(dimension_semantics=sem, vmem_limit_bytes=VMEM_LIMIT)


def _pick_tile(n, target, mult):
    best = None
    for t in range(mult, min(n, target) + 1, mult):
        if n % t == 0:
            best = t
    assert best is not None, (n, target, mult)
    return best


def _rms(x, g):
    return x * lax.rsqrt(jnp.mean(x * x, axis=-1, keepdims=True) + EPS) * g


def _inproj_kernel(x_ref, g_ref, w_ref, q_ref, kv_ref, hy_ref, gate_ref):
    xn = _rms(x_ref[...], g_ref[...]).astype(BF16)
    q_ref[...] = jnp.dot(xn, w_ref[:, Q0:K0], preferred_element_type=F32)
    kv_ref[...] = jnp.dot(xn, w_ref[:, K0:HY0], preferred_element_type=F32)
    hy_ref[...] = jnp.dot(xn, w_ref[:, HY0:GA0], preferred_element_type=F32)
    gate_ref[...] = jnp.dot(xn, w_ref[:, GA0:IN_COLS], preferred_element_type=F32)


def _inproj(h2, g, w):
    rows = h2.shape[0]
    tm = _pick_tile(rows, 256, 128)
    row = lambda w_: pl.BlockSpec((tm, w_), lambda i: (i, 0))
    return pl.pallas_call(
        _inproj_kernel,
        grid=(rows // tm,),
        in_specs=[row(D_MODEL), pl.BlockSpec((1, D_MODEL), lambda i: (0, 0)),
                  pl.BlockSpec((D_MODEL, IN_COLS), lambda i: (0, 0))],
        out_specs=[row(ATTN_WIDTH), row(2 * KV_WIDTH), row(HY_PROJ), row(2 * D_MODEL)],
        out_shape=[jax.ShapeDtypeStruct((rows, ATTN_WIDTH), F32),
                   jax.ShapeDtypeStruct((rows, 2 * KV_WIDTH), F32),
                   jax.ShapeDtypeStruct((rows, HY_PROJ), F32),
                   jax.ShapeDtypeStruct((rows, 2 * D_MODEL), F32)],
        compiler_params=_cparams(("parallel",)),
        name="inproj",
    )(h2, g, w)


def _rope(x, cos, sin_signed):
    width = x.shape[-1]
    reps = width // LANES
    if reps > 1:
        cos = jnp.concatenate([cos] * reps, axis=-1)
        sin_signed = jnp.concatenate([sin_signed] * reps, axis=-1)
    lane = lax.broadcasted_iota(jnp.int32, x.shape, 1)
    first_half = (lane % HEAD_DIM) < (HEAD_DIM // 2)
    rot = jnp.where(first_half, pltpu.roll(x, width - HEAD_DIM // 2, 1), pltpu.roll(x, HEAD_DIM // 2, 1))
    return x * cos + rot * sin_signed


def _attn_kernel(sink_ref, q_ref, kvp_ref, kvc_ref, kvn_ref, cq_ref, sq_ref, cp_ref, sp_ref, cn_ref, sn_ref,
                 o_ref, *, seq_len, padded_len):
    i = pl.program_id(1)
    nb = pl.num_programs(1)
    q = (_rope(q_ref[0], cq_ref[...], sq_ref[...]) * (HEAD_DIM ** -0.5)).astype(BF16)
    k = jnp.concatenate([
        _rope(kvp_ref[0, :, :KV_WIDTH], cp_ref[...], sp_ref[...]),
        _rope(kvc_ref[0, :, :KV_WIDTH], cq_ref[...], sq_ref[...]),
        _rope(kvn_ref[0, :, :KV_WIDTH], cn_ref[...], sn_ref[...])], axis=0).astype(BF16)
    v = jnp.concatenate([kvp_ref[0, :, KV_WIDTH:], kvc_ref[0, :, KV_WIDTH:], kvn_ref[0, :, KV_WIDTH:]],
                        axis=0).astype(BF16)
    rq = i * BLOCK + lax.broadcasted_iota(jnp.int32, (BLOCK, 1), 0)
    pos_q = (rq + N_META) % padded_len
    c = lax.broadcasted_iota(jnp.int32, (1, 3 * BLOCK), 1)
    blk = (i + nb - 1 + c // BLOCK) % nb
    pos_k = (blk * BLOCK + c % BLOCK + N_META) % padded_len
    mask = (jnp.abs(pos_k - pos_q) <= WINDOW) & (pos_k < seq_len)
    for h in range(N_HEADS):
        kh = h // GROUP
        qh = q[:, h * HEAD_DIM:(h + 1) * HEAD_DIM]
        kk = k[:, kh * HEAD_DIM:(kh + 1) * HEAD_DIM]
        vv = v[:, kh * HEAD_DIM:(kh + 1) * HEAD_DIM]
        s = lax.dot_general(qh, kk, (((1,), (1,)), ((), ())), preferred_element_type=F32)
        s = jnp.where(mask, s, NEG_BIG)
        sink = sink_ref[h]
        m = jnp.maximum(jnp.max(s, axis=-1, keepdims=True), sink)
        p = jnp.exp(s - m)
        denom = jnp.sum(p, axis=-1, keepdims=True) + jnp.exp(sink - m)
        o = jnp.dot(p.astype(BF16), vv, preferred_element_type=F32) / denom
        o_ref[0, :, h * HEAD_DIM:(h + 1) * HEAD_DIM] = o.astype(o_ref.dtype)


def _attention(q, kv, sink, cos_t, sin_t, seq_len):
    B, Lp, _ = q.shape
    nb = Lp // BLOCK
    assert nb >= 3
    kvspec = lambda d: pl.BlockSpec((1, BLOCK, 2 * KV_WIDTH), lambda b, i: (b, (i + nb + d) % nb, 0))
    tspec = lambda d: pl.BlockSpec((BLOCK, LANES), lambda b, i: ((i + nb + d) % nb, 0))
    return pl.pallas_call(
        functools.partial(_attn_kernel, seq_len=seq_len, padded_len=Lp),
        grid=(B, nb),
        in_specs=[pl.BlockSpec(memory_space=pltpu.SMEM),
                  pl.BlockSpec((1, BLOCK, ATTN_WIDTH), lambda b, i: (b, i, 0)),
                  kvspec(-1), kvspec(0), kvspec(1),
                  tspec(0), tspec(0), tspec(-1), tspec(-1), tspec(1), tspec(1)],
        out_specs=pl.BlockSpec((1, BLOCK, ATTN_WIDTH), lambda b, i: (b, i, 0)),
        out_shape=jax.ShapeDtypeStruct((B, Lp, ATTN_WIDTH), BF16),
        compiler_params=_cparams(("parallel", "parallel")),
        name="window_attention",
    )(sink, q, kv, kv, kv, cos_t, sin_t, cos_t, sin_t, cos_t, sin_t)


def _shortconv_kernel(uv_ref, u1_ref, u2_ref, wv_ref, w1_ref, w2_ref, bv_ref, b1_ref, b2_ref,
                      v_ref, vb_ref, x1_ref, x2_ref):
    def conv(u_ref, w_ref, b_ref):
        u = u_ref[0]
        n = u.shape[0]
        prev = pltpu.roll(u, 1, 0)
        nxt = pltpu.roll(u, n - 1, 0)
        return w_ref[0:1, :] * prev + w_ref[1:2, :] * u + w_ref[2:3, :] * nxt + b_ref[...]
    v = conv(uv_ref, wv_ref, bv_ref)
    v_ref[0] = v
    vb_ref[0] = v.astype(BF16)
    x1_ref[0] = conv(u1_ref, w1_ref, b1_ref)
    x2_ref[0] = conv(u2_ref, w2_ref, b2_ref)


def _shortconv(hy, conv_w, conv_b):
    B, Lp, _ = hy.shape
    tc = 128
    nct = HY_WIDTH // tc
    uspec = lambda part: pl.BlockSpec((1, Lp, tc), lambda b, j: (b, 0, part * nct + j))
    wspec = lambda part: pl.BlockSpec((3, tc), lambda b, j: (0, part * nct + j))
    bspec = lambda part: pl.BlockSpec((1, tc), lambda b, j: (0, part * nct + j))
    ospec = pl.BlockSpec((1, Lp, tc), lambda b, j: (b, 0, j))
    f32o = jax.ShapeDtypeStruct((B, Lp, HY_WIDTH), F32)
    return pl.pallas_call(
        _shortconv_kernel,
        grid=(B, nct),
        in_specs=[uspec(0), uspec(1), uspec(2), wspec(0), wspec(1), wspec(2), bspec(0), bspec(1), bspec(2)],
        out_specs=[ospec, ospec, ospec, ospec],
        out_shape=[f32o, jax.ShapeDtypeStruct((B, Lp, HY_WIDTH), BF16), f32o, f32o],
        compiler_params=_cparams(("parallel", "parallel")),
        name="hyena_shortconv",
    )(hy, hy, hy, conv_w, conv_w, conv_w, conv_b, conv_b, conv_b)


def _filter_kernel(z_ref, w1_ref, b1_ref, w2_ref, b2_ref, w3_ref, b3_ref, fr_ref, wo_ref, dl_ref,
                   e_ref, o_ref, *, seq_len, padded_len):
    hp = lax.Precision.HIGHEST
    i = pl.program_id(0)
    tr = z_ref.shape[0]
    z = z_ref[...]
    fr = fr_ref[...]
    a = jnp.sin(fr * (jnp.dot(z, w1_ref[...], precision=hp, preferred_element_type=F32) + b1_ref[...]))
    a = jnp.sin(fr * (jnp.dot(a, w2_ref[...], precision=hp, preferred_element_type=F32) + b2_ref[...]))
    a = jnp.sin(fr * (jnp.dot(a, w3_ref[...], precision=hp, preferred_element_type=F32) + b3_ref[...]))
    h = jnp.dot(a, wo_ref[...], precision=hp, preferred_element_type=F32)
    t = z[:, 0:1]
    decay = jnp.exp(-t * dl_ref[...])
    decay = jnp.concatenate([decay] * HY_ORDER, axis=-1)
    ow = HY_ORDER * HY_WIDTH
    hf = h[:, :ow] * decay
    hb = h[:, ow:] * decay
    lag = (i * tr + lax.broadcasted_iota(jnp.int32, (tr, 1), 0) + N_META) % padded_len
    valid = lag < seq_len
    zero_lag = lag == 0
    e_ref[...] = jnp.where(valid, hf + hb, 0.0)
    o_ref[...] = jnp.where(valid, jnp.where(zero_lag, hf + hb, hf - hb), 0.0)


def _filters(L, Lp, w1, b1, w2, b2, w3, b3, freq, wout):
    r = jnp.arange(Lp, dtype=jnp.int32)
    lag = ((r + N_META) % Lp).astype(F32)
    t = lag / max(L - 1, 1)
    f = jnp.linspace(1e-4, FILT_BANDS - 1, FILT_BANDS, dtype=F32)
    ang = (2.0 * math.pi / L) * lag[:, None] * f[None]
    z = jnp.concatenate([t[:, None], jnp.cos(ang), -jnp.sin(ang)], axis=-1)
    z = jnp.pad(z, ((0, 0), (0, LANES - FILT_EMB)))
    w1p = jnp.pad(w1, ((0, LANES - FILT_EMB), (0, 0)))
    max_decay = math.log(DECAY_TARGET) / FAST_DECAY_PCT
    min_decay = math.log(DECAY_TARGET) / SLOW_DECAY_PCT
    deltas = jnp.abs(jnp.linspace(min_decay, max_decay, HY_WIDTH, dtype=F32))[None]
    tr = _pick_tile(Lp, 512, 8)
    full = lambda a: pl.BlockSpec(a.shape, lambda i: (0,) * a.ndim)
    args = [w1p, b1[None], w2, b2[None], w3, b3[None], freq[None], wout, deltas]
    ow = HY_ORDER * HY_WIDTH
    return pl.pallas_call(
        functools.partial(_filter_kernel, seq_len=L, padded_len=Lp),
        grid=(Lp // tr,),
        in_specs=[pl.BlockSpec((tr, LANES), lambda i: (i, 0))] + [full(a) for a in args],
        out_specs=[pl.BlockSpec((tr, ow), lambda i: (i, 0))] * 2,
        out_shape=[jax.ShapeDtypeStruct((Lp, ow), F32)] * 2,
        compiler_params=_cparams(("parallel",)),
        name="hyena_filter_mlp",
    )(z, *args)


def _dft_mats(L, Lp):
    mh = Lp
    n = 2 * mh
    r = jnp.arange(Lp, dtype=jnp.int32)
    t = (r + N_META) % Lp
    valid = (t < L)[None, :]
    f = jnp.arange(mh, dtype=jnp.int32)[:, None]
    fb = 64
    assert mh % fb == 0
    angle = lambda k: ((k[:, None] * t[None, :]) % n).astype(F32) * (2.0 * math.pi / n)
    ang_h = angle(jnp.arange(mh // fb, dtype=jnp.int32) * fb)[:, None, :]
    ang_l = angle(jnp.arange(fb, dtype=jnp.int32))[None, :, :]
    ch, sh, cl, sl = jnp.cos(ang_h), jnp.sin(ang_h), jnp.cos(ang_l), jnp.sin(ang_l)
    c = (ch * cl - sh * sl).reshape(mh, Lp)
    s = (sh * cl + ch * sl).reshape(mh, Lp)
    nyq = jnp.where(t % 2 == 0, 1.0, -1.0).astype(F32)[None, :]
    fc = jnp.where(valid, c, 0.0)
    fs = jnp.where(valid, jnp.where(f == 0, nyq, -s), 0.0)
    wgt = jnp.where(f == 0, 1.0 / n, 2.0 / n).astype(F32)
    gc = (fc * wgt).T
    gs = (fs * wgt).T
    return fc.astype(BF16), fs.astype(BF16), gc.astype(BF16), gs.astype(BF16)


def _dft_plain_kernel(fc_ref, fs_ref, z_ref, re_ref, im_ref):
    z = z_ref[0]
    re_ref[0] = jnp.dot(fc_ref[...], z, preferred_element_type=F32)
    im_ref[0] = jnp.dot(fs_ref[...], z, preferred_element_type=F32)


def _dft_plain(fc, fs, z):
    nbatch, Lp, width = z.shape
    mh = fc.shape[0]
    tm = _pick_tile(mh, 512, 16)
    tn = 512
    ospec = pl.BlockSpec((1, tm, tn), lambda i, b, j: (b, i, j))
    osh = jax.ShapeDtypeStruct((nbatch, mh, width), F32)
    return pl.pallas_call(
        _dft_plain_kernel,
        grid=(mh // tm, nbatch, width // tn),
        in_specs=[pl.BlockSpec((tm, Lp), lambda i, b, j: (i, 0)), pl.BlockSpec((tm, Lp), lambda i, b, j: (i, 0)),
                  pl.BlockSpec((1, Lp, tn), lambda i, b, j: (b, 0, j))],
        out_specs=[ospec, ospec],
        out_shape=[osh, osh],
        compiler_params=_cparams(("parallel", "parallel", "parallel")),
        name="hyena_filter_dft",
    )(fc, fs, z)


def _dft_mul_kernel(fc_ref, fs_ref, z_ref, kr_ref, ki_ref, re_ref, im_ref):
    i = pl.program_id(0)
    z = z_ref[0]
    re = jnp.dot(fc_ref[...], z, preferred_element_type=F32)
    im = jnp.dot(fs_ref[...], z, preferred_element_type=F32)
    kr = kr_ref[...]
    ki = ki_ref[...]
    row0 = (i * re.shape[0] + lax.broadcasted_iota(jnp.int32, (re.shape[0], 1), 0)) == 0
    ki_x = jnp.where(row0, 0.0, ki)
    kd = jnp.where(row0, ki, kr)
    re_ref[0] = (re * kr - im * ki_x).astype(BF16)
    im_ref[0] = (re * ki_x + im * kd).astype(BF16)


def _dft_mul(fc, fs, zb, kr, ki, order):
    B, Lp, width = zb.shape
    mh = fc.shape[0]
    tm = _pick_tile(mh, 512, 16)
    ospec = pl.BlockSpec((1, tm, width), lambda i, b: (b, i, 0))
    osh = jax.ShapeDtypeStruct((B, mh, width), BF16)
    return pl.pallas_call(
        _dft_mul_kernel,
        grid=(mh // tm, B),
        in_specs=[pl.BlockSpec((tm, Lp), lambda i, b: (i, 0)), pl.BlockSpec((tm, Lp), lambda i, b: (i, 0)),
                  pl.BlockSpec((1, Lp, width), lambda i, b: (b, 0, 0)),
                  pl.BlockSpec((tm, width), lambda i, b: (i, order)),
                  pl.BlockSpec((tm, width), lambda i, b: (i, order))],
        out_specs=[ospec, ospec],
        out_shape=[osh, osh],
        compiler_params=_cparams(("parallel", "parallel")),
        name="hyena_dft_mul",
    )(fc, fs, zb, kr, ki)


def _idft_gate_kernel(gc_ref, gs_ref, re_ref, im_ref, gate_ref, z_ref, sk_ref, zo_ref, zb_ref):
    conv = jnp.dot(gc_ref[...], re_ref[0], preferred_element_type=F32)
    conv += jnp.dot(gs_ref[...], im_ref[0], preferred_element_type=F32)
    zn = gate_ref[0] * (conv + sk_ref[...] * z_ref[0])
    zo_ref[0] = zn
    zb_ref[0] = zn.astype(BF16)


def _idft_gate(gc, gs, sre, sim, gate, z, skip, order):
    B, mh, width = sre.shape
    Lp = gc.shape[0]
    tm = _pick_tile(Lp, 512, 16)
    rspec = pl.BlockSpec((1, tm, width), lambda i, b: (b, i, 0))
    sspec = pl.BlockSpec((1, mh, width), lambda i, b: (b, 0, 0))
    gspec = pl.BlockSpec((tm, mh), lambda i, b: (i, 0))
    return pl.pallas_call(
        _idft_gate_kernel,
        grid=(Lp // tm, B),
        in_specs=[gspec, gspec, sspec, sspec, rspec, rspec, pl.BlockSpec((1, width), lambda i, b: (0, 0))],
        out_specs=[rspec, rspec],
        out_shape=[jax.ShapeDtypeStruct((B, Lp, width), F32), jax.ShapeDtypeStruct((B, Lp, width), BF16)],
        compiler_params=_cparams(("parallel", "parallel")),
        name="hyena_idft_gate",
    )(gc, gs, sre, sim, gate, z, skip[order][None])


def _hyena(hy, L, conv_w, conv_b, fw1, fb1, fw2, fb2, fw3, fb3, freq, wout, skip):
    B, Lp, _ = hy.shape
    v, vb, x1, x2 = _shortconv(hy, conv_w, conv_b[None])
    fc, fs, gc, gs = _dft_mats(L, Lp)
    e, o = _filters(L, Lp, fw1, fb1, fw2, fb2, fw3, fb3, freq, wout)
    eo = jnp.stack([e, o])
    hi = eo.astype(BF16)
    lo = (eo - hi.astype(F32)).astype(BF16)
    sre, sim = _dft_plain(fc, fs, jnp.concatenate([hi, lo], axis=0))
    sre = sre[:2] + sre[2:]
    sim = sim[:2] + sim[2:]
    row0 = (jnp.arange(Lp) == 0)[:, None]
    kr = sre[0]
    ki = jnp.where(row0, sim[0], sim[1])
    z, zb = v, vb
    for order, gate in enumerate((x1, x2)):
        pre, pim = _dft_mul(fc, fs, zb, kr, ki, order)
        z, zb = _idft_gate(gc, gs, pre, pim, gate, z, skip, order)
    return zb


EXT_WIDTH = D_MODEL + LANES


def _merge_kernel(a_ref, ho_ref, gate_ref, h_ref, wa_ref, wh_ref, wo_ref, gf_ref, wr_ref, *rest):
    hnew_ref, ext_ref, afft_ref = rest[-3:]
    ga = jax.nn.sigmoid(gate_ref[:, :D_MODEL])
    gh = jax.nn.sigmoid(gate_ref[:, D_MODEL:])
    merged = ga * jnp.dot(a_ref[...], wa_ref[...], preferred_element_type=F32)
    merged += gh * jnp.dot(ho_ref[...], wh_ref[...], preferred_element_type=F32)
    h = h_ref[...] + jnp.dot(merged.astype(BF16), wo_ref[...], preferred_element_type=F32)
    hnew_ref[...] = h
    hn = _rms(h, gf_ref[...])
    logits = jnp.dot(hn, wr_ref[...], precision=lax.Precision.HIGHEST, preferred_element_type=F32)
    lane = lax.broadcasted_iota(jnp.int32, logits.shape, 1)
    logits = jnp.where(lane < N_EXPERTS, logits, NEG_BIG)
    m = jnp.max(logits, axis=-1, keepdims=True)
    p = jnp.exp(logits - m)
    aff = p / jnp.sum(p, axis=-1, keepdims=True)
    ext_ref[:, :D_MODEL] = hn
    ext_ref[:, D_MODEL:] = aff
    afft_ref[...] = aff.T[:N_EXPERTS]


def _merge(a2, ho2, gates, h2, wa, wh, wo, gf, wr, ext_prev, row_base, rows_total):
    rows = h2.shape[0]
    tm = _pick_tile(rows, 256, 128)
    assert row_base % tm == 0
    row = lambda w_: pl.BlockSpec((tm, w_), lambda i: (i, 0))
    full = lambda a: pl.BlockSpec(a.shape, lambda i: (0,) * a.ndim)
    args = [a2, ho2, gates, h2, wa, wh, wo, gf, wr]
    in_specs = [row(ATTN_WIDTH), row(HY_WIDTH), row(2 * D_MODEL), row(D_MODEL),
                full(wa), full(wh), full(wo), full(gf), full(wr)]
    aliases = {}
    if ext_prev is not None:
        args.append(ext_prev)
        in_specs.append(pl.BlockSpec(memory_space=pl.ANY))
        aliases = {len(args) - 1: 1}
    return pl.pallas_call(
        _merge_kernel,
        grid=(rows // tm,),
        in_specs=in_specs,
        out_specs=[row(D_MODEL), pl.BlockSpec((tm, EXT_WIDTH), lambda i: (i + row_base // tm, 0)),
                   pl.BlockSpec((N_EXPERTS, tm), lambda i: (0, i))],
        out_shape=[jax.ShapeDtypeStruct((rows, D_MODEL), F32), jax.ShapeDtypeStruct((rows_total, EXT_WIDTH), F32),
                   jax.ShapeDtypeStruct((N_EXPERTS, rows), F32)],
        input_output_aliases=aliases,
        compiler_params=_cparams(("arbitrary",)),
        name="merge_router",
    )(*args)


ROUTE_CHUNK = 256
SLOT_TILE = 128


def _select_kernel(aff_ref, triu_ref, tril_ref, incl_ref, dval_ref, start_ref, end_ref, bits_ref,
                   *, cap, seq_tokens, padded_len, y_base):
    rows = aff_ref.shape[1]
    ch = triu_ref.shape[0]
    lane = lax.broadcasted_iota(jnp.int32, (1, rows), 1)
    rl = lane % padded_len
    valid = (rl < seq_tokens) | (rl >= padded_len - N_META)
    bits_ref[...] = jnp.where(valid, aff_ref[...], -1.0)

    def count_ge(cand_bits):
        cand = pltpu.bitcast(cand_bits, F32)
        return jnp.sum(jnp.where(bits_ref[...] >= cand, 1.0, 0.0), axis=1, keepdims=True)

    def bisect(k, tb):
        cand = tb | jnp.left_shift(jnp.int32(1), 30 - k)
        return jnp.where(count_ge(cand) >= cap, cand, tb)

    thr = pltpu.bitcast(lax.fori_loop(0, 31, bisect, jnp.zeros((N_EXPERTS, 1), jnp.int32)), F32)
    need = cap - jnp.sum(jnp.where(bits_ref[...] > thr, 1.0, 0.0), axis=1, keepdims=True)
    triu = triu_ref[...]
    tril = tril_ref[...]

    def chunk(c, carry):
        ceq, csel, crow = carry
        sl = pl.ds(pl.multiple_of(c * ch, ch), ch)
        b = bits_ref[:, sl]
        eq = b == thr
        incl_eq = jnp.dot(jnp.where(eq, 1.0, 0.0).astype(BF16), triu, preferred_element_type=F32) + ceq
        sel = (b > thr) | (eq & (incl_eq <= need))
        self_ = jnp.where(sel, 1.0, 0.0)
        selb = self_.astype(BF16)
        incl = jnp.dot(selb, triu, preferred_element_type=F32) + csel
        kt = jnp.sum(self_, axis=0, keepdims=True)
        kt8 = jnp.broadcast_to(kt, (8, ch)).astype(BF16)
        excl = jnp.dot(kt8, triu, preferred_element_type=F32)[0:1] - kt + crow
        rank = jnp.dot(tril, selb, preferred_element_type=F32)
        incl_ref[:, sl] = jnp.where(sel, incl, -incl - 1.0)
        dval_ref[:, sl] = excl + rank
        start_ref[:, sl] = excl
        end_ref[:, sl] = excl + kt
        return (ceq + jnp.sum(jnp.where(eq, 1.0, 0.0), axis=1, keepdims=True),
                csel + jnp.sum(self_, axis=1, keepdims=True),
                crow + jnp.sum(kt, axis=1, keepdims=True))

    zero = jnp.zeros((N_EXPERTS, 1), F32)
    lax.fori_loop(0, rows // ch, chunk, (zero, zero, jnp.full((1, 1), y_base, F32)))


def _select(afft, cap, seq_tokens, padded_len, y_base):
    rows = afft.shape[1]
    ch = ROUTE_CHUNK
    assert rows % ch == 0
    triu = (jnp.arange(ch)[:, None] <= jnp.arange(ch)[None, :]).astype(BF16)
    tril = (jnp.arange(N_EXPERTS)[:, None] > jnp.arange(N_EXPERTS)[None, :]).astype(BF16)
    full = lambda shape: pl.BlockSpec(shape, lambda i: (0,) * len(shape))
    return pl.pallas_call(
        functools.partial(_select_kernel, cap=cap, seq_tokens=seq_tokens, padded_len=padded_len, y_base=y_base),
        grid=(1,),
        in_specs=[full((N_EXPERTS, rows)), full((ch, ch)), full((N_EXPERTS, N_EXPERTS))],
        out_specs=[full((N_EXPERTS, rows)), full((N_EXPERTS, rows)), full((1, rows)), full((1, rows))],
        out_shape=[jax.ShapeDtypeStruct((N_EXPERTS, rows), F32), jax.ShapeDtypeStruct((N_EXPERTS, rows), F32),
                   jax.ShapeDtypeStruct((1, rows), F32), jax.ShapeDtypeStruct((1, rows), F32)],
        scratch_shapes=[pltpu.VMEM((N_EXPERTS, rows), F32)],
        compiler_params=_cparams(("arbitrary",)),
        name="expert_select",
    )(afft, triu, tril)


def _lists_kernel(offs_ref, incl_ref, dval_ref, idx_ref, dst_ref, *, cap, nchunks, row_base, dump_base):
    ch = ROUTE_CHUNK
    t = SLOT_TILE
    cp = idx_ref.shape[0]
    slot_all = lax.broadcasted_iota(jnp.int32, (cp, LANES), 0)
    lane_all = lax.broadcasted_iota(jnp.int32, (cp, LANES), 1)
    idx_ref[...] = jnp.full((cp, LANES), row_base, jnp.int32)
    dst_ref[...] = dump_base + jnp.maximum(slot_all - cap, 0) * N_EXPERTS + jnp.minimum(lane_all, N_EXPERTS - 1)
    lane = lax.broadcasted_iota(jnp.int32, (t, LANES), 1)
    tok_lane = lax.broadcasted_iota(jnp.int32, (1, ch), 1)

    def chunk(c, _):
        csl = pl.ds(pl.multiple_of(c * ch, ch), ch)
        tok = (row_base + c * ch + tok_lane).astype(F32)
        for e in range(N_EXPERTS):
            o0 = offs_ref[e * (nchunks + 1) + c]
            o1 = offs_ref[e * (nchunks + 1) + c + 1]
            row_incl = incl_ref[e:e + 1, csl]
            row_dval = dval_ref[e:e + 1, csl]

            def tile(j, _):
                sl = pl.ds(pl.multiple_of(j * t, t), t)
                slot = j * t + lax.broadcasted_iota(jnp.int32, (t, 1), 0)
                hit = row_incl == (slot + 1).astype(F32)
                src = jnp.sum(jnp.where(hit, tok, 0.0), axis=1, keepdims=True).astype(jnp.int32)
                dst = jnp.sum(jnp.where(hit, row_dval, 0.0), axis=1, keepdims=True).astype(jnp.int32)
                m = (slot >= o0) & (slot < o1) & (lane == e)
                idx_ref[sl, :] = jnp.where(m, src, idx_ref[sl, :])
                dst_ref[sl, :] = jnp.where(m, dst, dst_ref[sl, :])
                return 0

            lax.fori_loop(o0 // t, (o1 + t - 1) // t, tile, 0)
        return 0

    lax.fori_loop(0, nchunks, chunk, 0)


def _lists(offs, incl, dval, cap, cp, row_base, dump_base):
    rows = incl.shape[1]
    nchunks = rows // ROUTE_CHUNK
    full = lambda shape: pl.BlockSpec(shape, lambda i, o: (0,) * len(shape))
    return pl.pallas_call(
        functools.partial(_lists_kernel, cap=cap, nchunks=nchunks, row_base=row_base, dump_base=dump_base),
        grid_spec=pltpu.PrefetchScalarGridSpec(
            num_scalar_prefetch=1, grid=(1,),
            in_specs=[full((N_EXPERTS, rows)), full((N_EXPERTS, rows))],
            out_specs=[full((cp, LANES)), full((cp, LANES))]),
        out_shape=[jax.ShapeDtypeStruct((cp, LANES), jnp.int32)] * 2,
        compiler_params=_cparams(("arbitrary",)),
        name="expert_slot_lists",
    )(offs, incl, dval)


FFN_TILE = 256


def _ffn_kernel(idx_hbm, dst_hbm, ext_hbm, wg_ref, wu_ref, wd_ref, y_hbm,
                xbuf, ybuf, idx_sm, dst_sm, gsem, ssem, isem, dsem):
    e = pl.program_id(0)
    i = pl.program_id(1)
    nt = pl.num_programs(1)
    total = pl.num_programs(0) * nt
    s = e * nt + i
    slot = s % 2
    tm = FFN_TILE

    def idx_copy(step, buf):
        return pltpu.make_async_copy(idx_hbm.at[step], idx_sm.at[buf], isem)

    def dst_copy(step):
        return pltpu.make_async_copy(dst_hbm.at[step], dst_sm, dsem)

    def gather_row(buf, r, priority=0):
        pltpu.make_async_copy(ext_hbm.at[pl.ds(idx_sm[buf, r], 1)], xbuf.at[buf, pl.ds(r, 1)],
                              gsem.at[buf]).start(priority=priority)

    def scatter_row(r, priority=0):
        pltpu.make_async_copy(ybuf.at[pl.ds(r, 1)], y_hbm.at[pl.ds(dst_sm[r], 1)], ssem).start(priority=priority)

    def gather_wait(buf):
        pltpu.make_async_copy(ext_hbm.at[pl.ds(0, tm)], xbuf.at[buf], gsem.at[buf]).wait()

    def scatter_wait():
        pltpu.make_async_copy(ybuf, y_hbm.at[pl.ds(0, tm)], ssem).wait()

    @pl.when(s == 0)
    def _():
        c = idx_copy(0, 0)
        c.start()
        c.wait()

        def body(r, _):
            gather_row(0, r)
            return 0
        lax.fori_loop(0, tm, body, 0, unroll=8)
        ybuf[...] = jnp.zeros_like(ybuf)

    nxt = jnp.minimum(s + 1, total - 1)
    ic = idx_copy(nxt, 1 - slot)
    dc = dst_copy(s)
    ic.start()
    dc.start()
    gather_wait(slot)
    ic.wait()
    dc.wait()

    parts = 4
    per = tm // parts

    def issue(part):
        for r in range(part * per, (part + 1) * per):
            gather_row(1 - slot, r, priority=r % 2)
            scatter_row(r, priority=r % 2)

    x = xbuf[slot]
    lane = lax.broadcasted_iota(jnp.int32, (tm, LANES), 1)
    gate = jnp.sum(jnp.where(lane == e, x[:, D_MODEL:], 0.0), axis=1, keepdims=True)
    xb = x[:, :D_MODEL].astype(BF16)
    issue(0)
    gt = jnp.dot(xb, wg_ref[0], preferred_element_type=F32)
    issue(1)
    up = jnp.dot(xb, wu_ref[0], preferred_element_type=F32)
    issue(2)
    hid = (gt * jax.nn.sigmoid(gt) * up).astype(BF16)
    y = jnp.dot(hid, wd_ref[0], preferred_element_type=F32) * gate
    issue(3)
    scatter_wait()
    ybuf[...] = y

    @pl.when(s == total - 1)
    def _():
        gather_wait(1 - slot)
        c = dst_copy(total)
        c.start()
        c.wait()

        def body(r, _):
            scatter_row(r)
            return 0
        lax.fori_loop(0, tm, body, 0, unroll=8)
        scatter_wait()


def _ffn(idx, dst, ext, wg, wu, wd, y_rows):
    E = wg.shape[0]
    nt = idx.shape[0] // E
    tm = FFN_TILE
    anyspec = pl.BlockSpec(memory_space=pl.ANY)
    return pl.pallas_call(
        _ffn_kernel,
        grid=(E, nt),
        in_specs=[anyspec, anyspec, anyspec,
                  pl.BlockSpec((1, D_MODEL, D_EXPERT), lambda e, i: (e, 0, 0)),
                  pl.BlockSpec((1, D_MODEL, D_EXPERT), lambda e, i: (e, 0, 0)),
                  pl.BlockSpec((1, D_EXPERT, D_MODEL), lambda e, i: (e, 0, 0))],
        out_specs=anyspec,
        out_shape=jax.ShapeDtypeStruct((y_rows, D_MODEL), F32),
        scratch_shapes=[pltpu.VMEM((2, tm, EXT_WIDTH), F32), pltpu.VMEM((tm, D_MODEL), F32),
                        pltpu.SMEM((2, tm), jnp.int32), pltpu.SMEM((tm,), jnp.int32),
                        pltpu.SemaphoreType.DMA((2,)), pltpu.SemaphoreType.DMA,
                        pltpu.SemaphoreType.DMA, pltpu.SemaphoreType.DMA],
        compiler_params=_cparams(("arbitrary", "arbitrary")),
        name="expert_ffn",
    )(idx, dst, ext, wg, wu, wd)


COMBINE_CHUNK = 128
Y_BLOCK = 256


def _combine_kernel(cg_ref, bi_ref, ci_ref, blk_ref, flag_ref, y_ref, start_ref, end_ref, h_ref, g_ref, o_ref, acc_ref,
                    *, y_limit):
    w = pl.program_id(0)
    flags = flag_ref[w]

    @pl.when((flags & 1) != 0)
    def _():
        acc_ref[...] = jnp.zeros_like(acc_ref)

    @pl.when((flags & 4) != 0)
    def _():
        tc = COMBINE_CHUNK
        st = jnp.broadcast_to(start_ref[...], (tc, tc)).T
        en = jnp.broadcast_to(end_ref[...], (tc, tc)).T
        reps = Y_BLOCK // tc
        st = jnp.concatenate([st] * reps, axis=1)
        en = jnp.concatenate([en] * reps, axis=1)
        jg = (blk_ref[w] * Y_BLOCK + lax.broadcasted_iota(jnp.int32, (1, Y_BLOCK), 1)).astype(F32)
        onehot = jnp.where((jg >= st) & (jg < en), 1.0, 0.0).astype(BF16)
        jrow = blk_ref[w] * Y_BLOCK + lax.broadcasted_iota(jnp.int32, (Y_BLOCK, 1), 0)
        y = jnp.where(jrow < y_limit, y_ref[...], 0.0)
        hi = y.astype(BF16)
        lo = (y - hi.astype(F32)).astype(BF16)
        acc_ref[...] += (jnp.dot(onehot, hi, preferred_element_type=F32)
                         + jnp.dot(onehot, lo, preferred_element_type=F32))

    @pl.when((flags & 2) != 0)
    def _():
        o_ref[0] = _rms(h_ref[...] + acc_ref[...], g_ref[...])


def _combine(work, y, start, end, hnew, g, B, S, y_limit):
    cg, bi, ci, blk, flags = work
    nw = cg.shape[0]
    tc = COMBINE_CHUNK
    assert S % tc == 0 and hnew.shape[0] % tc == 0
    return pl.pallas_call(
        functools.partial(_combine_kernel, y_limit=y_limit),
        grid_spec=pltpu.PrefetchScalarGridSpec(
            num_scalar_prefetch=5, grid=(nw,),
            in_specs=[pl.BlockSpec((Y_BLOCK, D_MODEL), lambda w, cg, bi, ci, blk, fl: (blk[w], 0)),
                      pl.BlockSpec((1, tc), lambda w, cg, bi, ci, blk, fl: (0, cg[w])),
                      pl.BlockSpec((1, tc), lambda w, cg, bi, ci, blk, fl: (0, cg[w])),
                      pl.BlockSpec((tc, D_MODEL), lambda w, cg, bi, ci, blk, fl: (cg[w], 0)),
                      pl.BlockSpec((1, D_MODEL), lambda w, cg, bi, ci, blk, fl: (0, 0))],
            out_specs=pl.BlockSpec((1, tc, D_MODEL), lambda w, cg, bi, ci, blk, fl: (bi[w], ci[w], 0)),
            scratch_shapes=[pltpu.VMEM((tc, D_MODEL), F32)]),
        out_shape=jax.ShapeDtypeStruct((B, S, D_MODEL), F32),
        compiler_params=_cparams(("arbitrary",)),
        name="combine_final_norm",
    )(cg, bi, ci, blk, flags, y, start, end, hnew, g)


def _combine_worklist(start, end, B, S, Lp, n_contrib):
    tc = COMBINE_CHUNK
    cpb = S // tc
    b = jnp.repeat(jnp.arange(B, dtype=jnp.int32), cpb)
    ci = jnp.tile(jnp.arange(cpb, dtype=jnp.int32), B)
    cg = b * (Lp // tc) + ci
    lo = start[0, cg * tc].astype(jnp.int32)
    hi = end[0, cg * tc + tc - 1].astype(jnp.int32)
    nblk = jnp.where(hi > lo, (hi - 1) // Y_BLOCK - lo // Y_BLOCK + 1, 1)
    cum = jnp.cumsum(nblk)
    nchunks = B * cpb
    nw = nchunks + n_contrib // Y_BLOCK + 2
    w = jnp.arange(nw, dtype=jnp.int32)
    valid = w < cum[-1]
    c = jnp.minimum(jnp.sum((cum[None, :] <= w[:, None]).astype(jnp.int32), axis=1), nchunks - 1)
    k = w - (cum[c] - nblk[c])
    k = jnp.where(valid, k, nblk[c] - 1)
    blk = lo[c] // Y_BLOCK + k
    flags = jnp.where(valid, (k == 0) * 1 + (k == nblk[c] - 1) * 2 + 4, 0).astype(jnp.int32)
    return cg[c], b[c], ci[c], blk.astype(jnp.int32), flags


def _rope_tables(Lp):
    half = HEAD_DIM // 2
    inv = 1.0 / (ROPE_THETA ** (jnp.arange(half, dtype=F32) / half))
    pos = ((jnp.arange(Lp, dtype=jnp.int32) + N_META) % Lp).astype(F32)
    ang = pos[:, None] * inv[None]
    cos = jnp.cos(ang)
    sin = jnp.sin(ang)
    reps = LANES // HEAD_DIM
    cos_t = jnp.concatenate([cos, cos] * reps, axis=-1)
    sin_t = jnp.concatenate([-sin, sin] * reps, axis=-1)
    return cos_t, sin_t


def _dims(x):
    B, S, _ = x.shape
    L = S + N_META
    Lp = (L // BLOCK + 1) * BLOCK
    return B, S, L, Lp


def _mixer(x, p, ext_prev, row_base, rows_total):
    B, S, L, Lp = _dims(x)
    meta = jnp.broadcast_to(p["meta_tokens"][None], (B, N_META, D_MODEL))
    h = jnp.concatenate([x, jnp.zeros((B, Lp - L, D_MODEL), F32), meta], axis=1)
    h2 = h.reshape(B * Lp, D_MODEL)
    q, kv, hy, gates = _inproj(h2, p["norm_mix_g"], p["w_in"])
    cos_t, sin_t = _rope_tables(Lp)
    a = _attention(q.reshape(B, Lp, -1), kv.reshape(B, Lp, -1), p["attn_sink"], cos_t, sin_t, L)
    ho = _hyena(hy.reshape(B, Lp, -1), L, p["hy_conv_w"], p["hy_conv_b"], p["hy_filt_w1"], p["hy_filt_b1"],
                p["hy_filt_w2"], p["hy_filt_b2"], p["hy_filt_w3"], p["hy_filt_b3"], p["hy_filt_freq"],
                p["hy_filt_wout"], p["hy_skip"])
    return _merge(a.reshape(B * Lp, -1), ho.reshape(B * Lp, -1), gates, h2,
                  p["w_attn_up"], p["w_hyena_up"], p["w_out"], p["norm_ffn_g"], p["w_router"],
                  ext_prev, row_base, rows_total)


def _route(afft, dims, row_base, y_base, dump_base):
    B, S, L, Lp = dims
    cap = max(1, (CAPACITY_FACTOR * B * L) // N_EXPERTS)
    cp = -(-cap // FFN_TILE) * FFN_TILE
    incl, dval, start, end = _select(afft, cap, S, Lp, y_base)
    last = incl[:, ROUTE_CHUNK - 1::ROUTE_CHUNK]
    counts = jnp.where(last >= 0, last, -last - 1.0).astype(jnp.int32)
    offs = jnp.concatenate([jnp.zeros((N_EXPERTS, 1), jnp.int32), counts], axis=1).reshape(-1)
    idx_t, dst_t = _lists(offs, incl, dval, cap, cp, row_base, dump_base)
    return idx_t[:, :N_EXPERTS].T, dst_t[:, :N_EXPERTS].T, start, end, cap, cp


def kernel(x_prompt, x_sample, meta_tokens, norm_mix_g, w_in, attn_sink, hy_conv_w, hy_conv_b, hy_filt_w1, hy_filt_b1, hy_filt_w2, hy_filt_b2, hy_filt_w3, hy_filt_b3, hy_filt_freq, hy_filt_wout, hy_skip, w_attn_up, w_hyena_up, w_out, norm_ffn_g, w_router, w_exp_gate, w_exp_up, w_exp_down, norm_final_g):
    p = dict(
        meta_tokens=meta_tokens, norm_mix_g=norm_mix_g[0][None], w_in=w_in[0].astype(BF16), attn_sink=attn_sink[0],
        hy_conv_w=hy_conv_w[0], hy_conv_b=hy_conv_b[0], hy_filt_w1=hy_filt_w1[0], hy_filt_b1=hy_filt_b1[0],
        hy_filt_w2=hy_filt_w2[0], hy_filt_b2=hy_filt_b2[0], hy_filt_w3=hy_filt_w3[0], hy_filt_b3=hy_filt_b3[0],
        hy_filt_freq=hy_filt_freq[0], hy_filt_wout=hy_filt_wout[0], hy_skip=hy_skip[0],
        w_attn_up=w_attn_up[0].astype(BF16), w_hyena_up=w_hyena_up[0].astype(BF16), w_out=w_out[0].astype(BF16),
        norm_ffn_g=norm_ffn_g[0][None],
        w_router=jnp.pad(w_router[0], ((0, 0), (0, ROUTER_PAD - N_EXPERTS))),
    )
    xs = (x_prompt, x_sample)
    dims = [_dims(x) for x in xs]
    rows = [B * Lp for (B, S, L, Lp) in dims]
    rows_total = sum(rows)
    caps = [max(1, (CAPACITY_FACTOR * B * L) // N_EXPERTS) for (B, S, L, Lp) in dims]
    cps = [-(-c // FFN_TILE) * FFN_TILE for c in caps]
    y_bases = [0, N_EXPERTS * caps[0]]
    n_contrib = N_EXPERTS * sum(caps)
    dump_bases = [n_contrib, n_contrib + N_EXPERTS * (cps[0] - caps[0])]
    y_rows = n_contrib + N_EXPERTS * sum(cp - c for cp, c in zip(cps, caps))
    spare_base = y_rows
    y_rows = -(-(y_rows + FFN_TILE) // Y_BLOCK) * Y_BLOCK

    ext = jnp.zeros((rows_total, EXT_WIDTH), F32)
    hnews, routes = [], []
    row_base = 0
    for g, x in enumerate(xs):
        hnew, ext, afft = _mixer(x, p, ext, row_base, rows_total)
        hnews.append(hnew)
        routes.append(_route(afft, dims[g], row_base, y_bases[g], dump_bases[g]))
        row_base += rows[g]
    idx = jnp.concatenate([r[0] for r in routes], axis=1).reshape(-1, FFN_TILE)
    dst = jnp.concatenate([r[1] for r in routes], axis=1).reshape(-1, FFN_TILE)
    dst = jnp.concatenate([spare_base + jnp.arange(FFN_TILE, dtype=jnp.int32)[None], dst], axis=0)
    dst = jnp.pad(dst, ((0, -dst.shape[0] % 8), (0, 0)))
    y = _ffn(idx, dst, ext, w_exp_gate[0].astype(BF16), w_exp_up[0].astype(BF16), w_exp_down[0].astype(BF16), y_rows)
    outs = []
    for g in range(len(xs)):
        B, S, L, Lp = dims[g]
        _, _, start, end, cap, _ = routes[g]
        work = _combine_worklist(start, end, B, S, Lp, N_EXPERTS * cap)
        outs.append(_combine(work, y, start, end, hnews[g], norm_final_g[None], B, S, n_contrib))
    return tuple(outs)
```
